```python
import math
import jax
import jax.numpy as jnp
from jax import lax
import numpy as np

D_MODEL = 1024
BATCH = 4
SEQ = 8192
DEPTH = 1
DEC_BATCH = 32
DEC_SEQ = 8
PAST_LEN = 16384
PAGE_SIZE = 128

A_GROUPS = ((128, 1), (512, 4), (2048, 16))
A_HEADS_PER_GROUP = 4
A_HEADS = A_HEADS_PER_GROUP * len(A_GROUPS)
A_HEAD_DIM = 128
A_WIDTH = A_HEADS * A_HEAD_DIM
A_OUT = A_HEADS_PER_GROUP * A_HEAD_DIM
REL_BUCKETS = 32
REL_MAX_DIST = 2048
B_HEADS = 4
B_DK = 128
B_DV = 256
B_QK = B_HEADS * B_DK
B_V = B_HEADS * B_DV
B_GATE_RANK = 16
B_GATE_NORM = 16.0
B_CHUNK = 64
MEM_LEN = 256
M_HEADS = 4
M_HEAD_DIM = D_MODEL // M_HEADS
M_WIDTH = M_HEADS * M_HEAD_DIM
D_FF = -(-8 * D_MODEL // (3 * 256)) * 256
IN_WIDTHS = (A_WIDTH, A_WIDTH, A_WIDTH, B_QK, B_QK, B_V, B_V, B_GATE_RANK, D_MODEL, D_MODEL)
EPS = 1e-6
NEG_INF = -1e30

kernel_name = 'hybrid_dilated_gla_decoder_step'


def rmsnorm(x, g):
    x32 = x.astype(jnp.float32)
    y = x32 * lax.rsqrt(jnp.mean(x32 * x32, axis=-1, keepdims=True) + EPS)
    return (y * g.astype(jnp.float32)).astype(x.dtype)


def t5_bucket(n):
    max_exact = REL_BUCKETS // 2
    nf = jnp.maximum(n, 1).astype(jnp.float32)
    large = max_exact + (jnp.log(nf / max_exact) / math.log(REL_MAX_DIST / max_exact)
                         * (REL_BUCKETS - max_exact)).astype(jnp.int32)
    return jnp.where(n < max_exact, n, jnp.minimum(large, REL_BUCKETS - 1))


def softmax_with_lse(s):
    mx = jnp.max(s, axis=-1, keepdims=True)
    p = jnp.exp(s - mx)
    den = jnp.sum(p, axis=-1, keepdims=True)
    return p / den, (mx + jnp.log(den))[..., 0]


def split_columns(t):
    parts, start = [], 0
    for w in IN_WIDTHS:
        parts.append(t[..., start:start + w])
        start += w
    return parts


def dilated_prompt(q, k, v, bias_tab, window, dil):
    bsz, seq, heads, dh = q.shape
    nk = window // dil
    span = dil * nk
    s_pad = -(-seq // span) * span
    m = s_pad // dil
    nb = m // nk

    def to_blocks(t):
        t = jnp.pad(t, ((0, 0), (0, s_pad - seq), (0, 0), (0, 0)))
        t = t.reshape(bsz, m, dil, heads, dh).transpose(0, 2, 1, 3, 4)
        return t.reshape(bsz, dil, nb, nk, heads, dh)

    def band(t):
        prev = jnp.pad(t[:, :, :-1], ((0, 0), (0, 0), (1, 0), (0, 0), (0, 0), (0, 0)))
        return jnp.concatenate([prev, t], axis=3)

    qb = to_blocks(q)
    kband = band(to_blocks(k))
    vband = band(to_blocks(v))
    i = jnp.arange(nk)[:, None]
    j = jnp.arange(2 * nk)[None, :]
    dsub = nk + i - j
    in_band = (dsub >= 0) & (dsub <= nk)
    blk = jnp.arange(nb)[:, None, None]
    valid = in_band[None] & (blk * nk + j[None] - nk >= 0)
    bias = bias_tab[t5_bucket(jnp.maximum(dsub, 0) * dil)].transpose(2, 0, 1)
    s = jnp.einsum('brnqhd,brnkhd->brnhqk', qb, kband).astype(jnp.float32) * (dh ** -0.5)
    s = s + bias.astype(jnp.float32)[None, None, None]
    s = jnp.where(valid[None, None, :, None], s, NEG_INF)
    p, lse = softmax_with_lse(s)
    o = jnp.einsum('brnhqk,brnkhd->brnqhd', p.astype(v.dtype), vband)
    o = o.reshape(bsz, dil, m, heads, dh).transpose(0, 2, 1, 3, 4).reshape(bsz, s_pad, heads, dh)[:, :seq]
    lse = lse.transpose(0, 1, 2, 4, 3).reshape(bsz, dil, m, heads).transpose(0, 2, 1, 3)
    lse = lse.reshape(bsz, s_pad, heads)[:, :seq]
    return o, lse


def dilated_sample(q, k, v, buf_k, buf_v, bias_tab, window, dil):
    bsz, seq, heads, dh = q.shape
    wb = buf_k.shape[1]
    nk = window // dil
    kc = jnp.concatenate([buf_k, k], axis=1)
    vc = jnp.concatenate([buf_v, v], axis=1)
    steps = jnp.arange(nk + 1)
    idx = wb + jnp.arange(seq)[:, None] - steps[None, :] * dil
    valid = idx >= 0
    idx = jnp.maximum(idx, 0)
    kg = kc[:, idx]
    vg = vc[:, idx]
    bias = bias_tab[t5_bucket(steps * dil)].T
    s = jnp.einsum('bqhd,bqkhd->bhqk', q, kg).astype(jnp.float32) * (dh ** -0.5)
    s = s + bias.astype(jnp.float32)[None, :, None, :]
    s = jnp.where(valid[None, None], s, NEG_INF)
    p, lse = softmax_with_lse(s)
    o = jnp.einsum('bhqk,bqkhd->bqhd', p.astype(v.dtype), vg)
    return o, lse.transpose(0, 2, 1)


def combine_dilations(outs, lses):
    w = jax.nn.softmax(jnp.stack(lses, axis=0), axis=0)
    o = jnp.sum(w[..., None] * jnp.stack(outs, axis=0).astype(jnp.float32), axis=0)
    return o.astype(outs[0].dtype)


def gla_chunked(q, k, v, logf, s0, chunk):
    bsz, seq, heads, dk = q.shape
    dv = v.shape[-1]
    nc = seq // chunk

    def prep(t):
        t = t.astype(jnp.float32).reshape(bsz, nc, chunk, heads, t.shape[-1])
        return t.transpose(1, 0, 3, 2, 4)

    qc = prep(q) * (dk ** -0.5)
    kc, vc, fc = prep(k), prep(v), prep(logf)
    mask = jnp.tril(jnp.ones((chunk, chunk), dtype=bool))
    mid = chunk // 2

    def step(state, inp):
        qx, kx, vx, fx = inp
        b = jnp.cumsum(fx, axis=2)
        bm = b[:, :, mid:mid + 1]
        blast = b[:, :, -1:]
        o_inter = jnp.einsum('bhcd,bhde->bhce', qx * jnp.exp(b), state)
        a = jnp.einsum('bhid,bhjd->bhij', qx * jnp.exp(b - bm), kx * jnp.exp(bm - b))
        a = jnp.where(mask, a, 0.0)
        o = o_inter + jnp.einsum('bhij,bhje->bhie', a, vx)
        state = jnp.exp(blast[:, :, 0])[..., None] * state + jnp.einsum('bhcd,bhce->bhde', kx * jnp.exp(blast - b), vx)
        return state, o

    s_fin, o = lax.scan(step, s0.astype(jnp.float32), (qc, kc, vc, fc))
    o = o.transpose(1, 0, 3, 2, 4).reshape(bsz, seq, heads, dv)
    return o, s_fin


def memory_kv(mem, g, w_mk, w_mv):
    bsz, mlen, _ = mem.shape
    m = rmsnorm(mem, g)
    k = (m @ w_mk).reshape(bsz, mlen, M_HEADS, M_HEAD_DIM)
    v = (m @ w_mv).reshape(bsz, mlen, M_HEADS, M_HEAD_DIM)
    return jnp.stack([k, v], axis=2)


def cross_attend(h, mem_kv, w_mq, w_mo):
    bsz, seq, _ = h.shape
    q = (h @ w_mq).reshape(bsz, seq, M_HEADS, M_HEAD_DIM)
    s = jnp.einsum('bqhd,bmhd->bhqm', q, mem_kv[:, :, 0]).astype(jnp.float32) * (M_HEAD_DIM ** -0.5)
    p = jax.nn.softmax(s, axis=-1).astype(h.dtype)
    o = jnp.einsum('bhqm,bmhd->bqhd', p, mem_kv[:, :, 1])
    return o.reshape(bsz, seq, M_WIDTH) @ w_mo


def trunk_layer(x, mem_kv, gla_s0, gla_chunk, win_bufs, rel_bias, lw):
    (n_mix_pre, n_mix_post, w_in, w_f2, b_f2, gla_norm, w_proj_a, w_proj_b, w_out,
     n_mem_pre, n_mem_post, w_mq, w_mo, n_ffn_pre, n_ffn_post, w_gate, w_up, w_down) = lw
    bsz, seq, _ = x.shape
    h = rmsnorm(x, n_mix_pre)
    qa, ka, va, qb, kb, vb, gb, fb, gate_a, gate_b = split_columns(h @ w_in)
    qa = qa.reshape(bsz, seq, A_HEADS, A_HEAD_DIM)
    ka = ka.reshape(bsz, seq, A_HEADS, A_HEAD_DIM)
    va = va.reshape(bsz, seq, A_HEADS, A_HEAD_DIM)
    outs, lses, win_new = [], [], []
    for gi, (window, dil) in enumerate(A_GROUPS):
        hs = slice(gi * A_HEADS_PER_GROUP, (gi + 1) * A_HEADS_PER_GROUP)
        tab = rel_bias[:, hs]
        qg, kg, vg = qa[:, :, hs], ka[:, :, hs], va[:, :, hs]
        if win_bufs is None:
            o, l = dilated_prompt(qg, kg, vg, tab, window, dil)
            keep = min(window, seq)
            win_new.append(jnp.stack([kg[:, seq - keep:], vg[:, seq - keep:]], axis=2))
        else:
            buf = win_bufs[gi]
            o, l = dilated_sample(qg, kg, vg, buf[:, :, 0], buf[:, :, 1], tab, window, dil)
            win_new.append(jnp.stack([kg, vg], axis=2))
        outs.append(o)
        lses.append(l)
    ya = combine_dilations(outs, lses).reshape(bsz, seq, A_OUT) @ w_proj_a
    qb = qb.reshape(bsz, seq, B_HEADS, B_DK)
    kb = kb.reshape(bsz, seq, B_HEADS, B_DK)
    vb = vb.reshape(bsz, seq, B_HEADS, B_DV)
    logf = jax.nn.log_sigmoid((fb @ w_f2 + b_f2).astype(jnp.float32)) / B_GATE_NORM
    logf = logf.reshape(bsz, seq, B_HEADS, B_DK)
    ob, s_new = gla_chunked(qb, kb, vb, logf, gla_s0, gla_chunk)
    ob = rmsnorm(ob.astype(x.dtype), gla_norm) * jax.nn.silu(gb).reshape(bsz, seq, B_HEADS, B_DV)
    yb = ob.reshape(bsz, seq, B_V) @ w_proj_b
    mix = (jax.nn.sigmoid(gate_a) * ya + jax.nn.sigmoid(gate_b) * yb) @ w_out
    x = x + rmsnorm(mix, n_mix_post)
    h = rmsnorm(x, n_mem_pre)
    x = x + rmsnorm(cross_attend(h, mem_kv, w_mq, w_mo), n_mem_post)
    h = rmsnorm(x, n_ffn_pre)
    ffn = (jax.nn.silu(h @ w_gate) * (h @ w_up)) @ w_down
    x = x + rmsnorm(ffn, n_ffn_post)
    return x, win_new, s_new.astype(x.dtype)


def setup_inputs(seed: int = 0) -> dict:
    key = jax.random.key(seed)
    ks = iter(jax.random.split(key, 32))
    f32 = jnp.float32

    def nrm(shape, scale=1.0):
        return jax.random.normal(next(ks), shape, f32) * scale

    def gain(shape):
        return 1.0 + 0.05 * nrm(shape)

    in_total = sum(IN_WIDTHS)
    wb = [min(w, PAST_LEN) for w, _ in A_GROUPS]
    return {
        'x_prompt': nrm((BATCH, SEQ, D_MODEL)),
        'x_sample': nrm((DEC_BATCH, DEC_SEQ, D_MODEL)),
        'cache_win1_kv': nrm((DEPTH, DEC_BATCH, wb[0], 2, A_HEADS_PER_GROUP, A_HEAD_DIM)),
        'cache_win2_kv': nrm((DEPTH, DEC_BATCH, wb[1], 2, A_HEADS_PER_GROUP, A_HEAD_DIM)),
        'cache_win3_kv': nrm((DEPTH, DEC_BATCH, wb[2], 2, A_HEADS_PER_GROUP, A_HEAD_DIM)),
        'state_gla': nrm((DEPTH, DEC_BATCH, B_HEADS, B_DK, B_DV), 0.5),
        'cache_mem_kv': nrm((DEPTH, DEC_BATCH, MEM_LEN, 2, M_HEADS, M_HEAD_DIM)),
        'mem_prompt': nrm((BATCH, MEM_LEN, D_MODEL)),
        'rel_bias': nrm((REL_BUCKETS, A_HEADS), 0.5),
        'norm_mix_pre': gain((DEPTH, D_MODEL)),
        'norm_mix_post': gain((DEPTH, D_MODEL)),
        'w_in': nrm((DEPTH, D_MODEL, in_total), D_MODEL ** -0.5),
        'w_f2': nrm((DEPTH, B_GATE_RANK, B_QK), B_GATE_RANK ** -0.5),
        'b_f2': nrm((DEPTH, B_QK), 0.1),
        'gla_norm': gain((DEPTH, B_DV)),
        'w_proj_a': nrm((DEPTH, A_OUT, D_MODEL), A_OUT ** -0.5),
        'w_proj_b': nrm((DEPTH, B_V, D_MODEL), B_V ** -0.5),
        'w_out': nrm((DEPTH, D_MODEL, D_MODEL), D_MODEL ** -0.5),
        'norm_memtok': gain((DEPTH, D_MODEL)),
        'w_mk': nrm((DEPTH, D_MODEL, M_WIDTH), D_MODEL ** -0.5),
        'w_mv': nrm((DEPTH, D_MODEL, M_WIDTH), D_MODEL ** -0.5),
        'norm_mem_pre': gain((DEPTH, D_MODEL)),
        'norm_mem_post': gain((DEPTH, D_MODEL)),
        'w_mq': nrm((DEPTH, D_MODEL, M_WIDTH), D_MODEL ** -0.5),
        'w_mo': nrm((DEPTH, M_WIDTH, D_MODEL), M_WIDTH ** -0.5),
        'norm_ffn_pre': gain((DEPTH, D_MODEL)),
        'norm_ffn_post': gain((DEPTH, D_MODEL)),
        'w_ffn_gate': nrm((DEPTH, D_MODEL, D_FF), D_MODEL ** -0.5),
        'w_ffn_up': nrm((DEPTH, D_MODEL, D_FF), D_MODEL ** -0.5),
        'w_ffn_down': nrm((DEPTH, D_FF, D_MODEL), D_FF ** -0.5),
    }


def reference(x_prompt, x_sample, cache_win1_kv, cache_win2_kv, cache_win3_kv, state_gla, cache_mem_kv,
              mem_prompt, rel_bias, norm_mix_pre, norm_mix_post, w_in, w_f2, b_f2, gla_norm, w_proj_a,
              w_proj_b, w_out, norm_memtok, w_mk, w_mv, norm_mem_pre, norm_mem_post, w_mq, w_mo,
              norm_ffn_pre, norm_ffn_post, w_ffn_gate, w_ffn_up, w_ffn_down):
    yp, ys = x_prompt, x_sample
    p_w1, p_w2, p_w3, p_gla, p_mem = [], [], [], [], []
    s_w1, s_w2, s_w3, s_gla = [], [], [], []
    for l in range(DEPTH):
        lw = (norm_mix_pre[l], norm_mix_post[l], w_in[l], w_f2[l], b_f2[l], gla_norm[l], w_proj_a[l],
              w_proj_b[l], w_out[l], norm_mem_pre[l], norm_mem_post[l], w_mq[l], w_mo[l],
              norm_ffn_pre[l], norm_ffn_post[l], w_ffn_gate[l], w_ffn_up[l], w_ffn_down[l])
        mem_kv_p = memory_kv(mem_prompt, norm_memtok[l], w_mk[l], w_mv[l])
        s0 = jnp.zeros((x_prompt.shape[0], B_HEADS, B_DK, B_DV), jnp.float32)
        yp, win_p, gla_p = trunk_layer(yp, mem_kv_p, s0, min(B_CHUNK, x_prompt.shape[1]), None, rel_bias, lw)
        bufs = (cache_win1_kv[l], cache_win2_kv[l], cache_win3_kv[l])
        ys, win_s, gla_s = trunk_layer(ys, cache_mem_kv[l], state_gla[l], x_sample.shape[1], bufs, rel_bias, lw)
        p_w1.append(win_p[0])
        p_w2.append(win_p[1])
        p_w3.append(win_p[2])
        p_gla.append(gla_p)
        p_mem.append(mem_kv_p)
        s_w1.append(win_s[0])
        s_w2.append(win_s[1])
        s_w3.append(win_s[2])
        s_gla.append(gla_s)
    return (yp, ys, jnp.stack(p_w1), jnp.stack(p_w2), jnp.stack(p_w3), jnp.stack(p_gla), jnp.stack(p_mem),
            jnp.stack(s_w1), jnp.stack(s_w2), jnp.stack(s_w3), jnp.stack(s_gla))
```

```python
import functools
import math

import numpy as np
import jax
import jax.numpy as jnp
from jax import lax
from jax.experimental import pallas as pl
from jax.experimental.pallas import tpu as pltpu

F32 = jnp.float32
BF16 = jnp.bfloat16

A_GROUPS = ((128, 1), (512, 4), (2048, 16))
A_HPG = 4
A_DH = 128
A_GW = A_HPG * A_DH
A_WIDTH = A_GW * len(A_GROUPS)
A_NK = 128
REL_BUCKETS = 32
REL_MAX_DIST = 2048
B_HEADS = 4
B_DK = 128
B_DV = 256
B_QK = B_HEADS * B_DK
B_V = B_HEADS * B_DV
B_RANK = 16
B_GATE_NORM = 16.0
B_CHUNK = 64
M_HEADS = 4
EPS = 1e-6
NEG_INF = -1e30

LANES = 128
VMEM_LIMIT_BYTES = 56 * 1024 * 1024

P_A = 3 * A_WIDTH
P_B = 2 * B_QK + 2 * B_V + LANES
P_FB_OFF = 2 * B_QK + 2 * B_V


def _cparams(sem):
    return pltpu.CompilerParams(dimension_semantics=sem, vmem_limit_bytes=VMEM_LIMIT_BYTES)


def _resident(shape):
    nd = len(shape)
    return pl.BlockSpec(shape, lambda *_: (0,) * nd, pipeline_mode=pl.Buffered(1))


def _rms(x, g):
    return x * lax.rsqrt(jnp.mean(x * x, axis=-1, keepdims=True) + EPS) * g


def _dot(a, b):
    return jnp.dot(a, b, preferred_element_type=F32)


def _dot_nt(a, b):
    return lax.dot_general(a, b, (((1,), (1,)), ((), ())), preferred_element_type=F32)


def _dot_tn(a, b):
    return lax.dot_general(a, b, (((0,), (0,)), ((), ())), preferred_element_type=F32)


def _in_proj_kernel(x_ref, g_ref, w_ref, oa_ref, ob_ref, og_ref, h_scr, *, chunk):
    h_scr[...] = _rms(x_ref[...], g_ref[...]).astype(BF16)
    base = 0
    for o_ref in (oa_ref, ob_ref, og_ref):
        width = o_ref.shape[1]
        for c0 in range(0, width, chunk):
            c1 = min(c0 + chunk, width)
            o_ref[:, c0:c1] = _dot(h_scr[...], w_ref[:, base + c0:base + c1]).astype(o_ref.dtype)
        base += width


def _in_proj(x, g, w, tm, out_dtype):
    m, d = x.shape
    widths = (P_A, P_B, w.shape[1] - P_A - P_B)
    return pl.pallas_call(
        functools.partial(_in_proj_kernel, chunk=512),
        grid=(m // tm,),
        in_specs=[pl.BlockSpec((tm, d), lambda i: (i, 0)), _resident(g.shape), _resident(w.shape)],
        out_specs=[pl.BlockSpec((tm, n), lambda i: (i, 0)) for n in widths],
        out_shape=[jax.ShapeDtypeStruct((m, n), out_dtype) for n in widths],
        scratch_shapes=[pltpu.VMEM((tm, d), BF16)],
        compiler_params=_cparams(("parallel",)),
        name="in_proj",
    )(x, g, w)


def _norm_matmul_kernel(x_ref, g_ref, w_ref, o_ref):
    o_ref[...] = _dot(_rms(x_ref[...], g_ref[...]).astype(BF16), w_ref[...]).astype(o_ref.dtype)


def _norm_matmul(x, g, w, tm):
    m, d = x.shape
    n = w.shape[1]
    return pl.pallas_call(
        _norm_matmul_kernel,
        grid=(m // tm,),
        in_specs=[pl.BlockSpec((tm, d), lambda i: (i, 0)), _resident(g.shape), _resident(w.shape)],
        out_specs=pl.BlockSpec((tm, n), lambda i: (i, 0)),
        out_shape=jax.ShapeDtypeStruct((m, n), F32),
        compiler_params=_cparams(("parallel",)),
        name="norm_matmul",
    )(x, g, w)


def _t5_bucket(n):
    max_exact = REL_BUCKETS // 2
    nf = jnp.maximum(n, 1).astype(F32)
    large = max_exact + (jnp.log(nf / max_exact) / math.log(REL_MAX_DIST / max_exact)
                         * (REL_BUCKETS - max_exact)).astype(jnp.int32)
    return jnp.where(n < max_exact, n, jnp.minimum(large, REL_BUCKETS - 1))


def _bias_by_steps(rel_bias, gi, dil):
    steps = jnp.arange(A_NK + 1, dtype=jnp.int32)
    return rel_bias[:, gi * A_HPG:(gi + 1) * A_HPG][_t5_bucket(steps * dil)].T.astype(F32)


def _masked_bias(tab, dsub, valid):
    b = jnp.take(tab, jnp.asarray(np.clip(dsub, 0, A_NK).reshape(-1)), axis=1).reshape((tab.shape[0],) + dsub.shape)
    return jnp.where(jnp.asarray(valid)[None], b, NEG_INF)


def _dil_prompt_kernel(q_ref, kc_ref, kp_ref, vc_ref, vp_ref, bias_ref, o_ref, l_ref, kbuf, vbuf, *, nsub):
    first = pl.program_id(2) == 0
    kbuf[0:A_NK, :] = kp_ref[0]
    kbuf[A_NK:, :] = kc_ref[0]
    vbuf[0:A_NK, :] = vp_ref[0]
    vbuf[A_NK:, :] = vc_ref[0]
    scale = A_DH ** -0.5
    for s in range(nsub):
        rows = slice(s * A_NK, (s + 1) * A_NK)
        win = slice(s * A_NK, (s + 2) * A_NK)
        for h in range(A_HPG):
            cols = slice(h * A_DH, (h + 1) * A_DH)
            sc = _dot_nt(q_ref[0, rows, cols], kbuf[win, cols]) * scale + bias_ref[h]
            if s == 0:
                col = lax.broadcasted_iota(jnp.int32, sc.shape, 1)
                sc = jnp.where(jnp.logical_and(first, col < A_NK), NEG_INF, sc)
            mx = jnp.max(sc, axis=-1, keepdims=True)
            p = jnp.exp(sc - mx)
            den = jnp.sum(p, axis=-1, keepdims=True)
            o = _dot(p.astype(BF16), vbuf[win, cols]) / den
            o_ref[0, rows, cols] = o.astype(o_ref.dtype)
            l_ref[0, rows, cols] = jnp.broadcast_to(mx + jnp.log(den), (A_NK, A_DH))


def _dil_prompt(qkv, bias, gi, dil, tq):
    bsz, seq, width = qkv.shape
    m = seq // dil
    tq = min(tq, m)
    nsub = tq // A_NK
    ncb = width // A_GW
    qv = qkv.reshape(bsz, m, dil * width)
    ncol = 3 * len(A_GROUPS)
    assert ncb == ncol

    def cur(off):
        return pl.BlockSpec((1, tq, A_GW), lambda b, r, i: (b, i, r * ncb + off + gi))

    def prev(off):
        return pl.BlockSpec((1, A_NK, A_GW), lambda b, r, i: (b, jnp.maximum(i * nsub - 1, 0), r * ncb + off + gi))

    out_spec = pl.BlockSpec((1, tq, A_GW), lambda b, r, i: (b, i, r))
    o, lse = pl.pallas_call(
        functools.partial(_dil_prompt_kernel, nsub=nsub),
        grid=(bsz, dil, m // tq),
        in_specs=[cur(0), cur(3), prev(3), cur(6), prev(6), _resident(bias.shape)],
        out_specs=[out_spec, out_spec],
        out_shape=[jax.ShapeDtypeStruct((bsz, m, dil * A_GW), qkv.dtype),
                   jax.ShapeDtypeStruct((bsz, m, dil * A_GW), F32)],
        scratch_shapes=[pltpu.VMEM((tq + A_NK, A_GW), qkv.dtype), pltpu.VMEM((tq + A_NK, A_GW), qkv.dtype)],
        compiler_params=_cparams(("parallel", "parallel", "arbitrary")),
        name=f"dil_prompt_g{gi}",
    )(qv, qv, qv, qv, qv, bias)
    return o.reshape(bsz * seq, A_GW), lse.reshape(bsz * seq, A_GW)


def _prompt_bias(rel_bias, gi, dil):
    a = np.arange(A_NK)[:, None]
    j = np.arange(2 * A_NK)[None, :]
    dsub = A_NK + a - j
    return _masked_bias(_bias_by_steps(rel_bias, gi, dil), dsub, (dsub >= 0) & (dsub <= A_NK))


def _dil_sample_kernel(qkv_ref, c1_ref, c2_ref, c3_ref, bc1_ref, bc2_ref, bc3_ref, bn_ref,
                       o1_ref, o2_ref, o3_ref, l1_ref, l2_ref, l3_ref, *, nres):
    scale = A_DH ** -0.5
    nq = qkv_ref.shape[1]
    outs = ((c1_ref, bc1_ref, o1_ref, l1_ref), (c2_ref, bc2_ref, o2_ref, l2_ref), (c3_ref, bc3_ref, o3_ref, l3_ref))
    for gi, (c_ref, bc_ref, o_ref, l_ref) in enumerate(outs):
        for h in range(A_HPG):
            cols = slice(gi * A_GW + h * A_DH, gi * A_GW + (h + 1) * A_DH)
            q = qkv_ref[0, :, cols].astype(BF16)
            knew = qkv_ref[0, :, A_WIDTH + cols.start:A_WIDTH + cols.stop].astype(BF16)
            vnew = qkv_ref[0, :, 2 * A_WIDTH + cols.start:2 * A_WIDTH + cols.stop].astype(BF16)
            row_w = 2 * A_GW
            nseg = nres if gi == 2 else 1
            nrow = c_ref.shape[1]
            ks = [c_ref[0, :, r * row_w + h * A_DH:r * row_w + (h + 1) * A_DH] for r in range(nseg)]
            vs = [c_ref[0, :, r * row_w + A_GW + h * A_DH:r * row_w + A_GW + (h + 1) * A_DH] for r in range(nseg)]
            scs = [_dot_nt(q, ks[r].astype(BF16)) * scale + bc_ref[h, :, r * nrow:(r + 1) * nrow] for r in range(nseg)]
            scs.append(_dot_nt(q, knew) * scale + bn_ref[gi, h])
            mx = functools.reduce(jnp.maximum, [jnp.max(s, axis=-1, keepdims=True) for s in scs])
            ps = [jnp.exp(s - mx) for s in scs]
            den = functools.reduce(jnp.add, [jnp.sum(p, axis=-1, keepdims=True) for p in ps])
            acc = _dot(ps[-1].astype(BF16), vnew)
            for r in range(nseg):
                acc = acc + _dot(ps[r].astype(BF16), vs[r].astype(BF16))
            ocols = slice(h * A_DH, (h + 1) * A_DH)
            o_ref[0, :, ocols] = (acc / den).astype(o_ref.dtype)
            l_ref[0, :, ocols] = jnp.broadcast_to(mx + jnp.log(den), (nq, A_DH))


def _sample_biases(rel_bias, nq):
    cache, new = [], []
    i = np.arange(nq)[:, None]
    for gi, (window, dil) in enumerate(A_GROUPS):
        tab = _bias_by_steps(rel_bias, gi, dil)
        wb = window
        r = np.arange(wb)[None, :]
        dist = wb + i - r
        bc = _masked_bias(tab, dist // dil, (dist % dil == 0) & (dist // dil <= A_NK))
        if gi == 2:
            nres = min(dil, nq)
            bc = bc.reshape(A_HPG, nq, wb // dil, dil).transpose(0, 1, 3, 2)[:, :, :nres].reshape(A_HPG, nq, nres * (wb // dil))
        cache.append(bc)
        dn = i - np.arange(nq)[None, :]
        new.append(_masked_bias(tab, dn // dil, (dn >= 0) & (dn % dil == 0)))
    return cache, jnp.stack(new)


def _dil_sample(qkv, caches, rel_bias):
    bsz, nq, width = qkv.shape
    bcs, bn = _sample_biases(rel_bias, nq)
    dil3 = A_GROUPS[2][1]
    nres = min(dil3, nq)
    w3 = caches[2].shape[1]
    assert w3 % dil3 == 0 and nq <= dil3, "new tokens must fall in distinct residue classes of the widest dilation"
    c3 = caches[2].reshape(bsz, w3 // dil3, dil3 * 2 * A_GW)
    row = lambda b: (b, 0, 0)
    out_spec = pl.BlockSpec((1, nq, A_GW), row)
    outs = pl.pallas_call(
        functools.partial(_dil_sample_kernel, nres=nres),
        grid=(bsz,),
        in_specs=[pl.BlockSpec((1, nq, width), row),
                  pl.BlockSpec((1,) + caches[0].shape[1:], row),
                  pl.BlockSpec((1,) + caches[1].shape[1:], row),
                  pl.BlockSpec((1, w3 // dil3, nres * 2 * A_GW), row),
                  _resident(bcs[0].shape), _resident(bcs[1].shape), _resident(bcs[2].shape), _resident(bn.shape)],
        out_specs=[out_spec] * 6,
        out_shape=[jax.ShapeDtypeStruct((bsz, nq, A_GW), F32)] * 6,
        compiler_params=_cparams(("parallel",)),
        name="dil_sample",
    )(qkv, caches[0], caches[1], c3, bcs[0], bcs[1], bcs[2], bn)
    return [o.reshape(bsz * nq, A_GW) for o in outs]


def _gla_kernel(q_ref, k_ref, v_ref, gb_ref, fb_ref, wf_ref, bf_ref, gn_ref, s0_ref, o_ref, sf_ref, st_scr,
                *, chunk, nchunks):
    t = pl.program_id(2)

    @pl.when(t == 0)
    def _():
        st_scr[...] = s0_ref[0, 0].T

    ri = lax.broadcasted_iota(jnp.int32, (chunk, chunk), 0)
    ci = lax.broadcasted_iota(jnp.int32, (chunk, chunk), 1)
    tril = ri >= ci
    tri = tril.astype(BF16)
    mid = chunk // 2
    for c in range(nchunks):
        rows = slice(c * chunk, (c + 1) * chunk)
        z = _dot(fb_ref[0, rows, :].astype(BF16), wf_ref[...]) + bf_ref[...]
        lf = (jnp.minimum(z, 0.0) - jnp.log(1.0 + jnp.exp(-jnp.abs(z)))) * (1.0 / B_GATE_NORM)
        hi = lf.astype(BF16)
        lo = (lf - hi.astype(F32)).astype(BF16)
        b = _dot(tri, hi) + _dot(tri, lo)
        bm = b[mid:mid + 1, :]
        bl = b[chunk - 1:chunk, :]
        q = q_ref[0, rows, :].astype(F32) * (B_DK ** -0.5)
        k = k_ref[0, rows, :].astype(F32)
        v = v_ref[0, rows, :].astype(BF16)
        st = st_scr[...]
        o = _dot_nt((q * jnp.exp(b)).astype(BF16), st.astype(BF16))
        a = _dot_nt((q * jnp.exp(b - bm)).astype(BF16), (k * jnp.exp(bm - b)).astype(BF16))
        a = jnp.where(tril, a, 0.0)
        o = o + _dot(a.astype(BF16), v)
        st_scr[...] = st * jnp.exp(bl) + _dot_tn(v, (k * jnp.exp(bl - b)).astype(BF16))
        gbv = gb_ref[0, rows, :].astype(F32)
        o_ref[0, rows, :] = (_rms(o, gn_ref[...]) * (gbv * jax.nn.sigmoid(gbv))).astype(o_ref.dtype)

    @pl.when(t == pl.num_programs(2) - 1)
    def _():
        sf_ref[0, 0] = st_scr[...].T


def _gla(pb, wf, bf, gn, s0, bsz, seq, chunk, tt):
    pb3 = pb.reshape(bsz, seq, P_B)
    tt = min(tt, seq)
    nk = B_QK // B_DK
    fb_blk = P_FB_OFF // LANES
    o, sf = pl.pallas_call(
        functools.partial(_gla_kernel, chunk=chunk, nchunks=tt // chunk),
        grid=(bsz, B_HEADS, seq // tt),
        in_specs=[pl.BlockSpec((1, tt, B_DK), lambda b, h, t: (b, t, h)),
                  pl.BlockSpec((1, tt, B_DK), lambda b, h, t: (b, t, nk + h)),
                  pl.BlockSpec((1, tt, B_DV), lambda b, h, t: (b, t, 2 * B_QK // B_DV + h)),
                  pl.BlockSpec((1, tt, B_DV), lambda b, h, t: (b, t, (2 * B_QK + B_V) // B_DV + h)),
                  pl.BlockSpec((1, tt, LANES), lambda b, h, t: (b, t, fb_blk)),
                  pl.BlockSpec((LANES, B_DK), lambda b, h, t: (0, h)),
                  pl.BlockSpec((1, B_DK), lambda b, h, t: (0, h)),
                  _resident(gn.shape),
                  pl.BlockSpec((1, 1, B_DK, B_DV), lambda b, h, t: (b, h, 0, 0))],
        out_specs=[pl.BlockSpec((1, tt, B_DV), lambda b, h, t: (b, t, h)),
                   pl.BlockSpec((1, 1, B_DK, B_DV), lambda b, h, t: (b, h, 0, 0))],
        out_shape=[jax.ShapeDtypeStruct((bsz, seq, B_V), pb.dtype),
                   jax.ShapeDtypeStruct((bsz, B_HEADS, B_DK, B_DV), F32)],
        scratch_shapes=[pltpu.VMEM((B_DV, B_DK), F32)],
        compiler_params=_cparams(("parallel", "parallel", "arbitrary")),
        name="gla",
    )(pb3, pb3, pb3, pb3, pb3, wf, bf, gn, s0)
    return o.reshape(bsz * seq, B_V), sf


def _mix_out_kernel(o1_ref, o2_ref, o3_ref, l1_ref, l2_ref, l3_ref, ob_ref, ga_ref, gbt_ref, x_ref,
                    wa_ref, wb_ref, wo_ref, gp_ref, y_ref):
    l1, l2, l3 = l1_ref[...], l2_ref[...], l3_ref[...]
    mx = jnp.maximum(jnp.maximum(l1, l2), l3)
    e1, e2, e3 = jnp.exp(l1 - mx), jnp.exp(l2 - mx), jnp.exp(l3 - mx)
    comb = (e1 * o1_ref[...].astype(F32) + e2 * o2_ref[...].astype(F32) + e3 * o3_ref[...].astype(F32)) / (e1 + e2 + e3)
    ya = _dot(comb.astype(BF16), wa_ref[...])
    yb = _dot(ob_ref[...].astype(BF16), wb_ref[...])
    mix = jax.nn.sigmoid(ga_ref[...].astype(F32)) * ya + jax.nn.sigmoid(gbt_ref[...].astype(F32)) * yb
    out = _dot(mix.astype(BF16), wo_ref[...])
    y_ref[...] = x_ref[...] + _rms(out, gp_ref[...])


def _mix_out(oas, lses, ob, gates, x, wa, wb, wo, gpost, tm):
    m, d = x.shape
    row = lambda i: (i, 0)
    return pl.pallas_call(
        _mix_out_kernel,
        grid=(m // tm,),
        in_specs=[pl.BlockSpec((tm, A_GW), row)] * 6
        + [pl.BlockSpec((tm, B_V), row), pl.BlockSpec((tm, d), row), pl.BlockSpec((tm, d), lambda i: (i, 1)),
           pl.BlockSpec((tm, d), row), _resident(wa.shape), _resident(wb.shape), _resident(wo.shape),
           _resident(gpost.shape)],
        out_specs=pl.BlockSpec((tm, d), row),
        out_shape=jax.ShapeDtypeStruct((m, d), F32),
        compiler_params=_cparams(("parallel",)),
        name="mix_out",
    )(*oas, *lses, ob, gates, gates, x, wa, wb, wo, gpost)


def _cross_kernel(x_ref, kv_ref, wq_ref, wo_ref, gpre_ref, gpost_ref, y_ref, q_scr, c_scr, *, nb):
    x = x_ref[...]
    d = x.shape[1]
    dh = d // M_HEADS
    q_scr[...] = _dot(_rms(x, gpre_ref[...]).astype(BF16), wq_ref[...])
    rpb = x.shape[0] // nb
    scale = dh ** -0.5
    for j in range(nb):
        rows = slice(j * rpb, (j + 1) * rpb)
        for h in range(M_HEADS):
            cols = slice(h * dh, (h + 1) * dh)
            k = kv_ref[j, :, cols].astype(BF16)
            v = kv_ref[j, :, d + cols.start:d + cols.stop].astype(BF16)
            sc = _dot_nt(q_scr[rows, cols].astype(BF16), k) * scale
            p = jnp.exp(sc - jnp.max(sc, axis=-1, keepdims=True))
            den = jnp.sum(p, axis=-1, keepdims=True)
            c_scr[rows, cols] = _dot(p.astype(BF16), v) / den
    out = _dot(c_scr[...].astype(BF16), wo_ref[...])
    y_ref[...] = x + _rms(out, gpost_ref[...])


def _cross(x, kv, wq, wo, gpre, gpost, tm, nb, rows_per_batch):
    m, d = x.shape
    tiles_per_kv = max(rows_per_batch // tm, 1)
    return pl.pallas_call(
        functools.partial(_cross_kernel, nb=nb),
        grid=(m // tm,),
        in_specs=[pl.BlockSpec((tm, d), lambda i: (i, 0)),
                  pl.BlockSpec((nb,) + kv.shape[1:], lambda i: (i // tiles_per_kv, 0, 0)),
                  _resident(wq.shape), _resident(wo.shape), _resident(gpre.shape), _resident(gpost.shape)],
        out_specs=pl.BlockSpec((tm, d), lambda i: (i, 0)),
        out_shape=jax.ShapeDtypeStruct((m, d), F32),
        scratch_shapes=[pltpu.VMEM((tm, d), F32), pltpu.VMEM((tm, d), F32)],
        compiler_params=_cparams(("parallel",)),
        name="cross_attn",
    )(x, kv, wq, wo, gpre, gpost)


def _ffn_kernel(x_ref, wg_ref, wu_ref, wd_ref, gpre_ref, gpost_ref, y_ref, h_scr, *, chunk):
    x = x_ref[...]
    h_scr[...] = _rms(x, gpre_ref[...]).astype(BF16)
    acc = jnp.zeros(x.shape, F32)
    for c0 in range(0, wg_ref.shape[1], chunk):
        g = _dot(h_scr[...], wg_ref[:, c0:c0 + chunk])
        u = _dot(h_scr[...], wu_ref[:, c0:c0 + chunk])
        acc = acc + _dot((g * jax.nn.sigmoid(g) * u).astype(BF16), wd_ref[c0:c0 + chunk, :])
    y_ref[...] = x + _rms(acc, gpost_ref[...])


def _ffn(x, wg, wu, wd, gpre, gpost, tm):
    m, d = x.shape
    return pl.pallas_call(
        functools.partial(_ffn_kernel, chunk=256),
        grid=(m // tm,),
        in_specs=[pl.BlockSpec((tm, d), lambda i: (i, 0)), _resident(wg.shape), _resident(wu.shape),
                  _resident(wd.shape), _resident(gpre.shape), _resident(gpost.shape)],
        out_specs=pl.BlockSpec((tm, d), lambda i: (i, 0)),
        out_shape=jax.ShapeDtypeStruct((m, d), F32),
        scratch_shapes=[pltpu.VMEM((tm, d), BF16)],
        compiler_params=_cparams(("parallel",)),
        name="ffn",
    )(x, wg, wu, wd, gpre, gpost)


def _row(v):
    return v.reshape(1, -1).astype(F32)


def _trunk(x, mem_kv, s0, chunk, caches, rel_bias, lw, *, tm, act_dtype, cross_tm, cross_nb):
    (g_mix_pre, g_mix_post, w_in, w_f2, b_f2, gla_norm, w_pa, w_pb, w_out,
     g_mem_pre, g_mem_post, w_mq, w_mo, g_ffn_pre, g_ffn_post, w_gate, w_up, w_down) = lw
    bsz, seq, d = x.shape
    x2 = x.reshape(bsz * seq, d)
    pa, pb, pg = _in_proj(x2, g_mix_pre, w_in, tm, act_dtype)
    pa3 = pa.reshape(bsz, seq, P_A)
    if caches is None:
        oas, lses = [], []
        for gi, (_, dil) in enumerate(A_GROUPS):
            o, l = _dil_prompt(pa3, _prompt_bias(rel_bias, gi, dil), gi, dil, 512)
            oas.append(o)
            lses.append(l)
        wins = []
        for gi, (window, _) in enumerate(A_GROUPS):
            keep = min(window, seq)
            kg = pa3[:, seq - keep:, A_WIDTH + gi * A_GW:A_WIDTH + (gi + 1) * A_GW]
            vg = pa3[:, seq - keep:, 2 * A_WIDTH + gi * A_GW:2 * A_WIDTH + (gi + 1) * A_GW]
            wins.append(jnp.stack([kg, vg], axis=2).astype(F32).reshape(bsz, keep, 2, A_HPG, A_DH))
    else:
        res = _dil_sample(pa3, [c.reshape(c.shape[0], c.shape[1], 2 * A_GW) for c in caches], rel_bias)
        oas, lses = res[:3], res[3:]
        wins = []
        for gi in range(len(A_GROUPS)):
            kg = pa3[:, :, A_WIDTH + gi * A_GW:A_WIDTH + (gi + 1) * A_GW]
            vg = pa3[:, :, 2 * A_WIDTH + gi * A_GW:2 * A_WIDTH + (gi + 1) * A_GW]
            wins.append(jnp.stack([kg, vg], axis=2).astype(F32).reshape(bsz, seq, 2, A_HPG, A_DH))
    ob, s_new = _gla(pb, w_f2, b_f2, gla_norm, s0, bsz, seq, chunk, 512)
    x2 = _mix_out(oas, lses, ob, pg, x2, w_pa, w_pb, w_out, g_mix_post, tm)
    x2 = _cross(x2, mem_kv, w_mq, w_mo, g_mem_pre, g_mem_post, cross_tm, cross_nb, seq)
    x2 = _ffn(x2, w_gate, w_up, w_down, g_ffn_pre, g_ffn_post, tm)
    return x2.reshape(bsz, seq, d), wins, s_new


def kernel(x_prompt, x_sample, cache_win1_kv, cache_win2_kv, cache_win3_kv, state_gla, cache_mem_kv, mem_prompt, rel_bias, norm_mix_pre, norm_mix_post, w_in, w_f2, b_f2, gla_norm, w_proj_a, w_proj_b, w_out, norm_memtok, w_mk, w_mv, norm_mem_pre, norm_mem_post, w_mq, w_mo, norm_ffn_pre, norm_ffn_post, w_ffn_gate, w_ffn_up, w_ffn_down):
    depth = w_in.shape[0]
    bsz, seq, d = x_prompt.shape
    dbs, dseq, _ = x_sample.shape
    mem_len = mem_prompt.shape[1]
    yp, ys = x_prompt, x_sample
    outs = [[] for _ in range(9)]
    for l in range(depth):
        wl = w_in[l]
        f0 = P_A + P_FB_OFF
        w_in_l = jnp.concatenate([wl[:, :f0], jnp.pad(wl[:, f0:f0 + B_RANK], ((0, 0), (0, LANES - B_RANK))),
                                  wl[:, f0 + B_RANK:]], axis=1).astype(BF16)
        w_f2_l = jnp.pad(w_f2[l], ((0, LANES - B_RANK), (0, 0))).astype(BF16)
        lw = (_row(norm_mix_pre[l]), _row(norm_mix_post[l]), w_in_l, w_f2_l, _row(b_f2[l]), _row(gla_norm[l]),
              w_proj_a[l].astype(BF16), w_proj_b[l].astype(BF16), w_out[l].astype(BF16),
              _row(norm_mem_pre[l]), _row(norm_mem_post[l]), w_mq[l].astype(BF16), w_mo[l].astype(BF16),
              _row(norm_ffn_pre[l]), _row(norm_ffn_post[l]),
              w_ffn_gate[l].astype(BF16), w_ffn_up[l].astype(BF16), w_ffn_down[l].astype(BF16))
        w_mkv = jnp.concatenate([w_mk[l], w_mv[l]], axis=1).astype(BF16)
        mem_kv_p = _norm_matmul(mem_prompt.reshape(bsz * mem_len, d), _row(norm_memtok[l]), w_mkv,
                                min(256, bsz * mem_len)).reshape(bsz, mem_len, 2 * d)
        s0 = jnp.zeros((bsz, B_HEADS, B_DK, B_DV), F32)
        tm_p = min(512, seq)
        yp, win_p, gla_p = _trunk(yp, mem_kv_p, s0, min(B_CHUNK, seq), None, rel_bias, lw,
                                  tm=tm_p, act_dtype=BF16, cross_tm=tm_p, cross_nb=1)
        caches = (cache_win1_kv[l], cache_win2_kv[l], cache_win3_kv[l])
        cross_nb = min(4, dbs)
        ys, win_s, gla_s = _trunk(ys, cache_mem_kv[l].reshape(dbs, mem_len, 2 * d), state_gla[l], dseq, caches,
                                  rel_bias, lw, tm=dbs * dseq, act_dtype=F32,
                                  cross_tm=cross_nb * dseq, cross_nb=cross_nb)
        for lst, v in zip(outs, (*win_p, gla_p, mem_kv_p.reshape(bsz, mem_len, 2, M_HEADS, d // M_HEADS), *win_s, gla_s)):
            lst.append(v)
    return (yp, ys, *[jnp.stack(o) for o in outs])
```

```python
import functools
import math

import numpy as np
import jax
import jax.numpy as jnp
from jax import lax
from jax.experimental import pallas as pl
from jax.experimental.pallas import tpu as pltpu

F32 = jnp.float32
BF16 = jnp.bfloat16

A_GROUPS = ((128, 1), (512, 4), (2048, 16))
A_HPG = 4
A_DH = 128
A_GW = A_HPG * A_DH
A_WIDTH = A_GW * len(A_GROUPS)
A_NK = 128
REL_BUCKETS = 32
REL_MAX_DIST = 2048
B_HEADS = 4
B_DK = 128
B_DV = 256
B_QK = B_HEADS * B_DK
B_V = B_HEADS * B_DV
B_RANK = 16
B_GATE_NORM = 16.0
B_CHUNK = 64
M_HEADS = 4
EPS = 1e-6
NEG_INF = -1e30

LANES = 128
SUBLANES = 8
VMEM_LIMIT_BYTES = 56 * 1024 * 1024

P_G = 3 * A_GW
P_B = 2 * B_QK + 2 * B_V + LANES
P_FB_OFF = 2 * B_QK + 2 * B_V


def _cparams(sem):
    return pltpu.CompilerParams(dimension_semantics=sem, vmem_limit_bytes=VMEM_LIMIT_BYTES)


def _resident(shape):
    nd = len(shape)
    return pl.BlockSpec(shape, lambda *_: (0,) * nd, pipeline_mode=pl.Buffered(1))


def _rms(x, g):
    return x * lax.rsqrt(jnp.mean(x * x, axis=-1, keepdims=True) + EPS) * g


def _dot(a, b):
    return jnp.dot(a, b, preferred_element_type=F32)


def _dot_nt(a, b):
    return lax.dot_general(a, b, (((1,), (1,)), ((), ())), preferred_element_type=F32)


def _dot_tn(a, b):
    return lax.dot_general(a, b, (((0,), (0,)), ((), ())), preferred_element_type=F32)


def _strided(start, size, stride):
    return pl.ds(start, size) if stride == 1 else pl.ds(start, size, stride=stride)


def _in_proj_kernel(x_ref, g_ref, w_ref, *refs, dils, chunk):
    ng = len(dils)
    a_refs, (pb_ref, pg_ref) = refs[:ng], refs[ng:ng + 2]
    h_scr, hs_scr = refs[ng + 2], refs[ng + 3]
    perm_scr = dict(zip([d for d in dils if d > 1], refs[ng + 4:]))
    tm, dm = x_ref.shape
    h = _rms(x_ref[...], g_ref[...])
    h_scr[...] = h.astype(BF16)
    if perm_scr:
        for s in range(dm // LANES):
            hs_scr[s] = h[:, s * LANES:(s + 1) * LANES]
        for d, hp in perm_scr.items():
            n = tm // d
            for s in range(dm // LANES):
                for r in range(d):
                    hp[r * n:(r + 1) * n, s * LANES:(s + 1) * LANES] = hs_scr[s, pl.ds(r, n, stride=d), :].astype(BF16)
    base = 0
    for a_ref, d in zip(a_refs, dils):
        lhs = h_scr if d == 1 else perm_scr[d]
        n = tm // d
        for c0 in range(0, P_G, chunk):
            res = _dot(lhs[...], w_ref[:, base + c0:base + c0 + chunk]).astype(a_ref.dtype)
            if d == 1:
                a_ref[:, c0:c0 + chunk] = res
            else:
                for r in range(d):
                    a_ref[0, r, :, c0:c0 + chunk] = res[r * n:(r + 1) * n]
        base += P_G
    for o_ref in (pb_ref, pg_ref):
        width = o_ref.shape[1]
        for c0 in range(0, width, chunk):
            c1 = min(c0 + chunk, width)
            o_ref[:, c0:c1] = _dot(h_scr[...], w_ref[:, base + c0:base + c1]).astype(o_ref.dtype)
        base += width


def _in_proj(x, g, w, bsz, seq, tm, dils, out_dtype):
    m, dm = x.shape
    tps = seq // tm
    pg_w = w.shape[1] - len(dils) * P_G - P_B
    out_specs, out_shape = [], []
    for d in dils:
        if d == 1:
            out_specs.append(pl.BlockSpec((tm, P_G), lambda i: (i, 0)))
            out_shape.append(jax.ShapeDtypeStruct((m, P_G), out_dtype))
        else:
            out_specs.append(pl.BlockSpec((1, d, tm // d, P_G), lambda i: (i // tps, 0, i % tps, 0)))
            out_shape.append(jax.ShapeDtypeStruct((bsz, d, seq // d, P_G), out_dtype))
    for n in (P_B, pg_w):
        out_specs.append(pl.BlockSpec((tm, n), lambda i: (i, 0)))
        out_shape.append(jax.ShapeDtypeStruct((m, n), out_dtype))
    perms = [d for d in dils if d > 1]
    scratch = [pltpu.VMEM((tm, dm), BF16), pltpu.VMEM((dm // LANES, tm, LANES) if perms else (1, SUBLANES, LANES), F32)]
    scratch += [pltpu.VMEM((tm, dm), BF16) for _ in perms]
    return pl.pallas_call(
        functools.partial(_in_proj_kernel, dils=tuple(dils), chunk=512),
        grid=(m // tm,),
        in_specs=[pl.BlockSpec((tm, dm), lambda i: (i, 0)), _resident(g.shape), _resident(w.shape)],
        out_specs=out_specs,
        out_shape=out_shape,
        scratch_shapes=scratch,
        compiler_params=_cparams(("parallel",)),
        name="in_proj",
    )(x, g, w)


def _norm_matmul_kernel(x_ref, g_ref, w_ref, o_ref):
    o_ref[...] = _dot(_rms(x_ref[...], g_ref[...]).astype(BF16), w_ref[...]).astype(o_ref.dtype)


def _norm_matmul(x, g, w, tm):
    m, d = x.shape
    n = w.shape[1]
    return pl.pallas_call(
        _norm_matmul_kernel,
        grid=(m // tm,),
        in_specs=[pl.BlockSpec((tm, d), lambda i: (i, 0)), _resident(g.shape), _resident(w.shape)],
        out_specs=pl.BlockSpec((tm, n), lambda i: (i, 0)),
        out_shape=jax.ShapeDtypeStruct((m, n), F32),
        compiler_params=_cparams(("parallel",)),
        name="norm_matmul",
    )(x, g, w)


def _t5_bucket(n):
    max_exact = REL_BUCKETS // 2
    nf = jnp.maximum(n, 1).astype(F32)
    large = max_exact + (jnp.log(nf / max_exact) / math.log(REL_MAX_DIST / max_exact)
                         * (REL_BUCKETS - max_exact)).astype(jnp.int32)
    return jnp.where(n < max_exact, n, jnp.minimum(large, REL_BUCKETS - 1))


def _bias_by_steps(rel_bias, gi, dil):
    steps = jnp.arange(A_NK + 1, dtype=jnp.int32)
    return rel_bias[:, gi * A_HPG:(gi + 1) * A_HPG][_t5_bucket(steps * dil)].T.astype(F32)


def _dil_prompt_kernel(q_ref, kc_ref, kp_ref, vc_ref, vp_ref, u_ref, o_ref, l_ref, bias_scr, of_scr, lf_scr,
                       *, dil, nsub):
    first = pl.program_id(1) == 0
    for h in range(A_HPG):
        bias_scr[h] = pltpu.roll(jnp.broadcast_to(u_ref[h:h + 1, :], (A_NK, 2 * A_NK)), 0, 1, stride=1, stride_axis=0)
    scale = A_DH ** -0.5
    col = lax.broadcasted_iota(jnp.int32, (A_NK, 2 * A_NK), 1)

    def block(idx, carry):
        if nsub == 1:
            r, s = idx, 0
        elif dil == 1:
            r, s = 0, idx
        else:
            r, s = idx // nsub, idx % nsub
        row0 = s * A_NK if isinstance(s, int) else pl.multiple_of(s * A_NK, A_NK)
        prow0 = 0 if isinstance(s, int) else pl.multiple_of(jnp.maximum(s - 1, 0) * A_NK, A_NK)
        is0 = s == 0
        no_prev = first if isinstance(s, int) else jnp.logical_and(first, is0)
        for h in range(A_HPG):
            cols = slice(h * A_DH, (h + 1) * A_DH)
            kprev, vprev = kp_ref[0, r, :, cols], vp_ref[0, r, :, cols]
            if not isinstance(s, int):
                kprev = jnp.where(is0, kprev, kc_ref[0, r, pl.ds(prow0, A_NK), cols])
                vprev = jnp.where(is0, vprev, vc_ref[0, r, pl.ds(prow0, A_NK), cols])
            kw = jnp.concatenate([kprev, kc_ref[0, r, pl.ds(row0, A_NK), cols]], axis=0)
            vw = jnp.concatenate([vprev, vc_ref[0, r, pl.ds(row0, A_NK), cols]], axis=0)
            sc = _dot_nt(q_ref[0, r, pl.ds(row0, A_NK), cols], kw) * scale + bias_scr[h]
            sc = jnp.where(jnp.logical_and(no_prev, col < A_NK), NEG_INF, sc)
            mx = jnp.max(sc, axis=-1, keepdims=True)
            p = jnp.exp(sc - mx)
            den = jnp.sum(p, axis=-1, keepdims=True)
            o = _dot(p.astype(BF16), vw) / den
            dst = _strided(row0 * dil + r, A_NK, dil)
            of_scr[h, dst, :] = o
            lf_scr[h, dst, :] = jnp.broadcast_to(mx + jnp.log(den), (A_NK, A_DH))
        return carry

    lax.fori_loop(0, dil * nsub, block, 0)
    for h in range(A_HPG):
        cols = slice(h * A_DH, (h + 1) * A_DH)
        o_ref[0, :, cols] = of_scr[h].astype(o_ref.dtype)
        l_ref[0, :, cols] = lf_scr[h]


def _dil_prompt(a, u, gi, dil, bsz, seq, tt):
    m = seq // dil
    tt = min(tt, seq)
    mt = tt // dil
    nsub = mt // A_NK
    assert mt % A_NK == 0 and seq % tt == 0

    def cur(cb):
        return pl.BlockSpec((1, dil, mt, A_GW), lambda b, i: (b, 0, i, cb))

    def prev(cb):
        return pl.BlockSpec((1, dil, A_NK, A_GW), lambda b, i: (b, 0, jnp.maximum(i * nsub - 1, 0), cb))

    out_spec = pl.BlockSpec((1, tt, A_GW), lambda b, i: (b, i, 0))
    o, lse = pl.pallas_call(
        functools.partial(_dil_prompt_kernel, dil=dil, nsub=nsub),
        grid=(bsz, seq // tt),
        in_specs=[cur(0), cur(1), prev(1), cur(2), prev(2), _resident(u.shape)],
        out_specs=[out_spec, out_spec],
        out_shape=[jax.ShapeDtypeStruct((bsz, seq, A_GW), a.dtype), jax.ShapeDtypeStruct((bsz, seq, A_GW), F32)],
        scratch_shapes=[pltpu.VMEM((A_HPG, A_NK, 2 * A_NK), F32), pltpu.VMEM((A_HPG, tt, A_DH), F32),
                        pltpu.VMEM((A_HPG, tt, A_DH), F32)],
        compiler_params=_cparams(("parallel", "arbitrary")),
        name=f"dil_prompt_g{gi}",
    )(a, a, a, a, a, u)
    return o.reshape(bsz * seq, A_GW), lse.reshape(bsz * seq, A_GW)


def _prompt_bias_row(rel_bias, gi, dil):
    tab = _bias_by_steps(rel_bias, gi, dil)
    return jnp.concatenate([tab[:, ::-1], jnp.full((A_HPG, A_NK - 1), NEG_INF, F32)], axis=1)


def _dil_sample_kernel(a1_ref, a2_ref, a3_ref, c1_ref, c2_ref, c3_ref, bc1_ref, bc2_ref, bc3_ref, bn_ref,
                       o1_ref, o2_ref, o3_ref, l1_ref, l2_ref, l3_ref):
    scale = A_DH ** -0.5
    nq = a1_ref.shape[1]
    groups = ((a1_ref, c1_ref, bc1_ref, o1_ref, l1_ref), (a2_ref, c2_ref, bc2_ref, o2_ref, l2_ref),
              (a3_ref, c3_ref, bc3_ref, o3_ref, l3_ref))
    for gi, (a_ref, c_ref, bc_ref, o_ref, l_ref) in enumerate(groups):
        rows = c_ref[0]
        if rows.ndim == 3:
            rows = rows.reshape(rows.shape[0] * rows.shape[1], rows.shape[2])
        rows = rows.astype(BF16)
        qs = [a_ref[0, :, h * A_DH:(h + 1) * A_DH] for h in range(A_HPG)]
        q_all = jnp.concatenate(qs, axis=0).astype(BF16)
        sc = _dot_nt(q_all, rows) * scale + bc_ref[...]
        scn = [_dot_nt(qs[h].astype(BF16), a_ref[0, :, A_GW + h * A_DH:A_GW + (h + 1) * A_DH].astype(BF16)) * scale
               + bn_ref[gi, h] for h in range(A_HPG)]
        scn = jnp.concatenate(scn, axis=0)
        mx = jnp.maximum(jnp.max(sc, axis=-1, keepdims=True), jnp.max(scn, axis=-1, keepdims=True))
        p = jnp.exp(sc - mx)
        pn = jnp.exp(scn - mx)
        den = jnp.sum(p, axis=-1, keepdims=True) + jnp.sum(pn, axis=-1, keepdims=True)
        acc = _dot(pltpu.roll(p, A_HPG, 1).astype(BF16), rows)
        lse = mx + jnp.log(den)
        for h in range(A_HPG):
            hr = slice(h * nq, (h + 1) * nq)
            vnew = a_ref[0, :, 2 * A_GW + h * A_DH:2 * A_GW + (h + 1) * A_DH].astype(BF16)
            o = (acc[hr] + _dot(pn[hr].astype(BF16), vnew)) / den[hr]
            o_ref[0, :, h * A_DH:(h + 1) * A_DH] = o.astype(o_ref.dtype)
            l_ref[0, :, h * A_DH:(h + 1) * A_DH] = jnp.broadcast_to(lse[hr], (nq, A_DH))


def _sample_biases(rel_bias, nq, nres):
    cache, new = [], []
    for gi, (window, dil) in enumerate(A_GROUPS):
        tab = _bias_by_steps(rel_bias, gi, dil)
        length = window + nq
        rep = jnp.repeat(tab, dil, axis=1)
        rep = jnp.where((jnp.arange(rep.shape[1]) % dil == 0)[None], rep, NEG_INF)
        by = jnp.concatenate([rep, jnp.full((A_HPG, max(length - rep.shape[1], 0)), NEG_INF, F32)], axis=1)[:, :length]
        rev = by[:, ::-1]
        bc = jnp.stack([rev[:, nq - 1 - i:nq - 1 - i + window] for i in range(nq)], axis=1)
        bn = jnp.stack([jnp.concatenate([rev[:, length - 1 - i:], jnp.full((A_HPG, nq - 1 - i), NEG_INF, F32)], axis=1)
                        for i in range(nq)], axis=1)
        own = (jnp.arange(2 * A_HPG)[None, :] == jnp.arange(A_HPG)[:, None])
        full = jnp.where(own[:, None, None, :], bc[:, :, :, None], NEG_INF)
        if gi == len(A_GROUPS) - 1:
            full = full.reshape(A_HPG, nq, window // dil, dil, 2 * A_HPG)[:, :, :, :nres]
        cache.append(full.reshape(A_HPG * nq, -1))
        new.append(bn)
    return cache, jnp.stack(new)


def _dil_sample(a_s, caches, rel_bias):
    bsz, nq, _ = a_s[0].shape
    w3, dil3 = caches[2].shape[1], A_GROUPS[2][1]
    nres = min(dil3, nq)
    assert w3 % dil3 == 0 and nq <= dil3, "new tokens must fall in distinct residue classes of the widest dilation"
    bcs, bn = _sample_biases(rel_bias, nq, nres)
    rpp = 2 * A_HPG
    c1 = caches[0].reshape(bsz, caches[0].shape[1] * rpp, A_DH)
    c2 = caches[1].reshape(bsz, caches[1].shape[1] * rpp, A_DH)
    c3 = caches[2].reshape(bsz, w3 // dil3, dil3 * rpp, A_DH)
    row = lambda b: (b, 0, 0)
    out_spec = pl.BlockSpec((1, nq, A_GW), row)
    outs = pl.pallas_call(
        _dil_sample_kernel,
        grid=(bsz,),
        in_specs=[pl.BlockSpec((1, nq, P_G), row)] * 3
        + [pl.BlockSpec((1,) + c1.shape[1:], row), pl.BlockSpec((1,) + c2.shape[1:], row),
           pl.BlockSpec((1, w3 // dil3, nres * rpp, A_DH), lambda b: (b, 0, 0, 0)),
           _resident(bcs[0].shape), _resident(bcs[1].shape), _resident(bcs[2].shape), _resident(bn.shape)],
        out_specs=[out_spec] * 6,
        out_shape=[jax.ShapeDtypeStruct((bsz, nq, A_GW), F32)] * 6,
        compiler_params=_cparams(("parallel",)),
        name="dil_sample",
    )(*a_s, c1, c2, c3, bcs[0], bcs[1], bcs[2], bn)
    return [o.reshape(bsz * nq, A_GW) for o in outs]


def _chunk_cumsum(x, chunk):
    pos = lax.broadcasted_iota(jnp.int32, x.shape, 0) % chunk
    sh = 1
    while sh < chunk:
        x = x + jnp.where(pos >= sh, pltpu.roll(x, sh, 0), 0.0)
        sh *= 2
    return x


def _gla_kernel(q_ref, k_ref, v_ref, gb_ref, fb_ref, wf_ref, bf_ref, gn_ref, s0_ref, o_ref, sf_ref, st_scr,
                *, chunk, nchunks):
    t = pl.program_id(1)

    @pl.when(t == 0)
    def _():
        st_scr[...] = s0_ref[0]

    tt = chunk * nchunks
    mid = chunk // 2
    z = _dot(fb_ref[0].astype(BF16), wf_ref[...]) + bf_ref[...]
    lf = (jnp.minimum(z, 0.0) - jnp.log(1.0 + jnp.exp(-jnp.abs(z)))) * (1.0 / B_GATE_NORM)
    b = _chunk_cumsum(lf, chunk)
    b3 = b.reshape(nchunks, chunk, B_QK)
    bm = jnp.broadcast_to(b3[:, mid:mid + 1, :], b3.shape).reshape(tt, B_QK)
    blr = b3[:, chunk - 1, :]
    bl = jnp.broadcast_to(b3[:, chunk - 1:chunk, :], b3.shape).reshape(tt, B_QK)
    q = q_ref[0].astype(F32) * (B_DK ** -0.5)
    k = k_ref[0].astype(F32)
    qe = (q * jnp.exp(b)).astype(BF16)
    qm = (q * jnp.exp(b - bm)).astype(BF16)
    km = (k * jnp.exp(bm - b)).astype(BF16)
    kl = (k * jnp.exp(bl - b)).astype(BF16)
    dec_t = jnp.exp(blr).T if nchunks >= SUBLANES else None
    ri = lax.broadcasted_iota(jnp.int32, (tt, tt), 0)
    ci = lax.broadcasted_iota(jnp.int32, (tt, tt), 1)
    causal = jnp.logical_and(ri >= ci, ri // chunk == ci // chunk)
    ones_col = jnp.ones((chunk, 1), F32)
    for h in range(B_HEADS):
        hs = slice(h * B_DK, (h + 1) * B_DK)
        vs = slice(h * B_DV, (h + 1) * B_DV)
        v = v_ref[0, :, vs].astype(BF16)
        a = jnp.where(causal, _dot_nt(qm[:, hs], km[:, hs]), 0.0)
        o_intra = _dot(a.astype(BF16), v)
        gbv = gb_ref[0, :, vs].astype(F32)
        gate = gbv * jax.nn.sigmoid(gbv)
        st = st_scr[h]
        for c in range(nchunks):
            rows = slice(c * chunk, (c + 1) * chunk)
            o = o_intra[rows] + _dot(qe[rows, hs], st.astype(BF16))
            if dec_t is not None:
                dcol = dec_t[hs, c:c + 1]
            else:
                dcol = jnp.exp(_dot_tn(lf[rows, hs], ones_col))
            st = st * dcol + _dot_tn(kl[rows, hs], v[rows])
            o_ref[0, rows, vs] = (_rms(o, gn_ref[...]) * gate[rows]).astype(o_ref.dtype)
        st_scr[h] = st

    @pl.when(t == pl.num_programs(1) - 1)
    def _():
        sf_ref[0] = st_scr[...]


def _gla(pb, wf, bf, gn, s0, bsz, seq, chunk, tt):
    pb3 = pb.reshape(bsz, seq, P_B)
    tt = min(tt, seq)
    o, sf = pl.pallas_call(
        functools.partial(_gla_kernel, chunk=chunk, nchunks=tt // chunk),
        grid=(bsz, seq // tt),
        in_specs=[pl.BlockSpec((1, tt, B_QK), lambda b, t: (b, t, 0)),
                  pl.BlockSpec((1, tt, B_QK), lambda b, t: (b, t, 1)),
                  pl.BlockSpec((1, tt, B_V), lambda b, t: (b, t, 2 * B_QK // B_V)),
                  pl.BlockSpec((1, tt, B_V), lambda b, t: (b, t, 2 * B_QK // B_V + 1)),
                  pl.BlockSpec((1, tt, LANES), lambda b, t: (b, t, P_FB_OFF // LANES)),
                  _resident(wf.shape), _resident(bf.shape), _resident(gn.shape),
                  pl.BlockSpec((1, B_HEADS, B_DK, B_DV), lambda b, t: (b, 0, 0, 0))],
        out_specs=[pl.BlockSpec((1, tt, B_V), lambda b, t: (b, t, 0)),
                   pl.BlockSpec((1, B_HEADS, B_DK, B_DV), lambda b, t: (b, 0, 0, 0))],
        out_shape=[jax.ShapeDtypeStruct((bsz, seq, B_V), pb.dtype),
                   jax.ShapeDtypeStruct((bsz, B_HEADS, B_DK, B_DV), F32)],
        scratch_shapes=[pltpu.VMEM((B_HEADS, B_DK, B_DV), F32)],
        compiler_params=_cparams(("parallel", "arbitrary")),
        name="gla",
    )(pb3, pb3, pb3, pb3, pb3, wf, bf, gn, s0)
    return o.reshape(bsz * seq, B_V), sf


def _mix_out_kernel(o1_ref, o2_ref, o3_ref, l1_ref, l2_ref, l3_ref, ob_ref, ga_ref, gbt_ref, x_ref,
                    wa_ref, wb_ref, wo_ref, gp_ref, y_ref):
    l1, l2, l3 = l1_ref[...], l2_ref[...], l3_ref[...]
    mx = jnp.maximum(jnp.maximum(l1, l2), l3)
    e1, e2, e3 = jnp.exp(l1 - mx), jnp.exp(l2 - mx), jnp.exp(l3 - mx)
    comb = (e1 * o1_ref[...].astype(F32) + e2 * o2_ref[...].astype(F32) + e3 * o3_ref[...].astype(F32)) / (e1 + e2 + e3)
    ya = _dot(comb.astype(BF16), wa_ref[...])
    yb = _dot(ob_ref[...].astype(BF16), wb_ref[...])
    mix = jax.nn.sigmoid(ga_ref[...].astype(F32)) * ya + jax.nn.sigmoid(gbt_ref[...].astype(F32)) * yb
    out = _dot(mix.astype(BF16), wo_ref[...])
    y_ref[...] = x_ref[...] + _rms(out, gp_ref[...])


def _mix_out(oas, lses, ob, gates, x, wa, wb, wo, gpost, tm):
    m, d = x.shape
    row = lambda i: (i, 0)
    return pl.pallas_call(
        _mix_out_kernel,
        grid=(m // tm,),
        in_specs=[pl.BlockSpec((tm, A_GW), row)] * 6
        + [pl.BlockSpec((tm, B_V), row), pl.BlockSpec((tm, d), row), pl.BlockSpec((tm, d), lambda i: (i, 1)),
           pl.BlockSpec((tm, d), row), _resident(wa.shape), _resident(wb.shape), _resident(wo.shape),
           _resident(gpost.shape)],
        out_specs=pl.BlockSpec((tm, d), row),
        out_shape=jax.ShapeDtypeStruct((m, d), F32),
        compiler_params=_cparams(("parallel",)),
        name="mix_out",
    )(*oas, *lses, ob, gates, gates, x, wa, wb, wo, gpost)


def _cross_kernel(x_ref, kv_ref, wq_ref, wo_ref, gpre_ref, gpost_ref, y_ref, q_scr, c_scr, *, nb):
    x = x_ref[...]
    d = x.shape[1]
    dh = d // M_HEADS
    q_scr[...] = _dot(_rms(x, gpre_ref[...]).astype(BF16), wq_ref[...])
    rpb = x.shape[0] // nb
    scale = dh ** -0.5
    for j in range(nb):
        rows = slice(j * rpb, (j + 1) * rpb)
        for h in range(M_HEADS):
            cols = slice(h * dh, (h + 1) * dh)
            k = kv_ref[j, :, cols].astype(BF16)
            v = kv_ref[j, :, d + cols.start:d + cols.stop].astype(BF16)
            sc = _dot_nt(q_scr[rows, cols].astype(BF16), k) * scale
            p = jnp.exp(sc - jnp.max(sc, axis=-1, keepdims=True))
            den = jnp.sum(p, axis=-1, keepdims=True)
            c_scr[rows, cols] = _dot(p.astype(BF16), v) / den
    out = _dot(c_scr[...].astype(BF16), wo_ref[...])
    y_ref[...] = x + _rms(out, gpost_ref[...])


def _cross(x, kv, wq, wo, gpre, gpost, tm, nb, rows_per_batch):
    m, d = x.shape
    tiles_per_kv = max(rows_per_batch // tm, 1)
    return pl.pallas_call(
        functools.partial(_cross_kernel, nb=nb),
        grid=(m // tm,),
        in_specs=[pl.BlockSpec((tm, d), lambda i: (i, 0)),
                  pl.BlockSpec((nb,) + kv.shape[1:], lambda i: (i // tiles_per_kv, 0, 0)),
                  _resident(wq.shape), _resident(wo.shape), _resident(gpre.shape), _resident(gpost.shape)],
        out_specs=pl.BlockSpec((tm, d), lambda i: (i, 0)),
        out_shape=jax.ShapeDtypeStruct((m, d), F32),
        scratch_shapes=[pltpu.VMEM((tm, d), F32), pltpu.VMEM((tm, d), F32)],
        compiler_params=_cparams(("parallel",)),
        name="cross_attn",
    )(x, kv, wq, wo, gpre, gpost)


def _ffn_kernel(x_ref, wg_ref, wu_ref, wd_ref, gpre_ref, gpost_ref, y_ref, h_scr, *, chunk):
    x = x_ref[...]
    h_scr[...] = _rms(x, gpre_ref[...]).astype(BF16)
    acc = jnp.zeros(x.shape, F32)
    for c0 in range(0, wg_ref.shape[1], chunk):
        g = _dot(h_scr[...], wg_ref[:, c0:c0 + chunk])
        u = _dot(h_scr[...], wu_ref[:, c0:c0 + chunk])
        acc = acc + _dot((g * jax.nn.sigmoid(g) * u).astype(BF16), wd_ref[c0:c0 + chunk, :])
    y_ref[...] = x + _rms(acc, gpost_ref[...])


def _ffn(x, wg, wu, wd, gpre, gpost, tm):
    m, d = x.shape
    return pl.pallas_call(
        functools.partial(_ffn_kernel, chunk=256),
        grid=(m // tm,),
        in_specs=[pl.BlockSpec((tm, d), lambda i: (i, 0)), _resident(wg.shape), _resident(wu.shape),
                  _resident(wd.shape), _resident(gpre.shape), _resident(gpost.shape)],
        out_specs=pl.BlockSpec((tm, d), lambda i: (i, 0)),
        out_shape=jax.ShapeDtypeStruct((m, d), F32),
        scratch_shapes=[pltpu.VMEM((tm, d), BF16)],
        compiler_params=_cparams(("parallel",)),
        name="ffn",
    )(x, wg, wu, wd, gpre, gpost)


def _row(v):
    return v.reshape(1, -1).astype(F32)


def _kv_rows(a, keep):
    bsz, seq, _ = a.shape
    return a[:, seq - keep:, A_GW:].astype(F32).reshape(bsz, keep, 2, A_HPG, A_DH)


def _trunk(x, mem_kv, s0, chunk, caches, rel_bias, lw, *, tm, act_dtype, cross_tm, cross_nb):
    (g_mix_pre, g_mix_post, w_in, w_f2, b_f2, gla_norm, w_pa, w_pb, w_out,
     g_mem_pre, g_mem_post, w_mq, w_mo, g_ffn_pre, g_ffn_post, w_gate, w_up, w_down) = lw
    bsz, seq, d = x.shape
    x2 = x.reshape(bsz * seq, d)
    if caches is None:
        dils = tuple(dil for _, dil in A_GROUPS)
        *a_g, pb, pg = _in_proj(x2, g_mix_pre, w_in, bsz, seq, tm, dils, act_dtype)
        oas, lses, wins = [], [], []
        for gi, (window, dil) in enumerate(A_GROUPS):
            a4 = a_g[gi].reshape(bsz, dil, seq // dil, P_G)
            o, l = _dil_prompt(a4, _prompt_bias_row(rel_bias, gi, dil), gi, dil, bsz, seq, A_NK * A_GROUPS[-1][1])
            oas.append(o)
            lses.append(l)
            keep = min(window, seq)
            tail = a4[:, :, (seq - keep) // dil:]
            wins.append(_kv_rows(tail.transpose(0, 2, 1, 3).reshape(bsz, keep, P_G), keep))
    else:
        *a_g, pb, pg = _in_proj(x2, g_mix_pre, w_in, bsz, seq, tm, (1,) * len(A_GROUPS), act_dtype)
        a_s = [a.reshape(bsz, seq, P_G) for a in a_g]
        res = _dil_sample(a_s, caches, rel_bias)
        oas, lses = res[:3], res[3:]
        wins = [_kv_rows(a, seq) for a in a_s]
    ob, s_new = _gla(pb, w_f2, b_f2, gla_norm, s0, bsz, seq, chunk, 512)
    x2 = _mix_out(oas, lses, ob, pg, x2, w_pa, w_pb, w_out, g_mix_post, tm)
    x2 = _cross(x2, mem_kv, w_mq, w_mo, g_mem_pre, g_mem_post, cross_tm, cross_nb, seq)
    x2 = _ffn(x2, w_gate, w_up, w_down, g_ffn_pre, g_ffn_post, tm)
    return x2.reshape(bsz, seq, d), wins, s_new


def kernel(x_prompt, x_sample, cache_win1_kv, cache_win2_kv, cache_win3_kv, state_gla, cache_mem_kv, mem_prompt, rel_bias, norm_mix_pre, norm_mix_post, w_in, w_f2, b_f2, gla_norm, w_proj_a, w_proj_b, w_out, norm_memtok, w_mk, w_mv, norm_mem_pre, norm_mem_post, w_mq, w_mo, norm_ffn_pre, norm_ffn_post, w_ffn_gate, w_ffn_up, w_ffn_down):
    depth = w_in.shape[0]
    bsz, seq, d = x_prompt.shape
    dbs, dseq, _ = x_sample.shape
    mem_len = mem_prompt.shape[1]
    yp, ys = x_prompt, x_sample
    outs = [[] for _ in range(9)]
    ng = len(A_GROUPS)
    for l in range(depth):
        wl = w_in[l]
        f0 = 3 * A_WIDTH + P_FB_OFF
        cols = [wl[:, part * A_WIDTH + gi * A_GW:part * A_WIDTH + (gi + 1) * A_GW] for gi in range(ng) for part in range(3)]
        cols += [wl[:, 3 * A_WIDTH:f0], jnp.pad(wl[:, f0:f0 + B_RANK], ((0, 0), (0, LANES - B_RANK))), wl[:, f0 + B_RANK:]]
        w_in_l = jnp.concatenate(cols, axis=1).astype(BF16)
        w_f2_l = jnp.pad(w_f2[l], ((0, LANES - B_RANK), (0, 0))).astype(BF16)
        lw = (_row(norm_mix_pre[l]), _row(norm_mix_post[l]), w_in_l, w_f2_l, _row(b_f2[l]), _row(gla_norm[l]),
              w_proj_a[l].astype(BF16), w_proj_b[l].astype(BF16), w_out[l].astype(BF16),
              _row(norm_mem_pre[l]), _row(norm_mem_post[l]), w_mq[l].astype(BF16), w_mo[l].astype(BF16),
              _row(norm_ffn_pre[l]), _row(norm_ffn_post[l]),
              w_ffn_gate[l].astype(BF16), w_ffn_up[l].astype(BF16), w_ffn_down[l].astype(BF16))
        w_mkv = jnp.concatenate([w_mk[l], w_mv[l]], axis=1).astype(BF16)
        mem_kv_p = _norm_matmul(mem_prompt.reshape(bsz * mem_len, d), _row(norm_memtok[l]), w_mkv,
                                min(256, bsz * mem_len)).reshape(bsz, mem_len, 2 * d)
        s0 = jnp.zeros((bsz, B_HEADS, B_DK, B_DV), F32)
        tm_p = min(512, seq)
        yp, win_p, gla_p = _trunk(yp, mem_kv_p, s0, min(B_CHUNK, seq), None, rel_bias, lw,
                                  tm=tm_p, act_dtype=BF16, cross_tm=tm_p, cross_nb=1)
        caches = (cache_win1_kv[l], cache_win2_kv[l], cache_win3_kv[l])
        cross_nb = min(4, dbs)
        ys, win_s, gla_s = _trunk(ys, cache_mem_kv[l].reshape(dbs, mem_len, 2 * d), state_gla[l], dseq, caches,
                                  rel_bias, lw, tm=dbs * dseq, act_dtype=F32,
                                  cross_tm=cross_nb * dseq, cross_nb=cross_nb)
        for lst, v in zip(outs, (*win_p, gla_p, mem_kv_p.reshape(bsz, mem_len, 2, M_HEADS, d // M_HEADS), *win_s, gla_s)):
            lst.append(v)
    return (yp, ys, *[jnp.stack(o) for o in outs])
```

```python
import functools
import math

import numpy as np
import jax
import jax.numpy as jnp
from jax import lax
from jax.experimental import pallas as pl
from jax.experimental.pallas import tpu as pltpu

F32 = jnp.float32
BF16 = jnp.bfloat16

A_GROUPS = ((128, 1), (512, 4), (2048, 16))
A_HPG = 4
A_DH = 128
A_GW = A_HPG * A_DH
A_WIDTH = A_GW * len(A_GROUPS)
A_NK = 128
REL_BUCKETS = 32
REL_MAX_DIST = 2048
B_HEADS = 4
B_DK = 128
B_DV = 256
B_QK = B_HEADS * B_DK
B_V = B_HEADS * B_DV
B_RANK = 16
B_GATE_NORM = 16.0
B_CHUNK = 64
M_HEADS = 4
EPS = 1e-6
NEG_INF = -1e30

LANES = 128
SUBLANES = 8
VMEM_LIMIT_BYTES = 56 * 1024 * 1024

P_G = 3 * A_GW
P_B = 2 * B_QK + 2 * B_V + LANES
P_FB_OFF = 2 * B_QK + 2 * B_V


def _cparams(sem):
    return pltpu.CompilerParams(dimension_semantics=sem, vmem_limit_bytes=VMEM_LIMIT_BYTES)


def _resident(shape):
    nd = len(shape)
    return pl.BlockSpec(shape, lambda *_: (0,) * nd, pipeline_mode=pl.Buffered(1))


def _rms(x, g):
    return x * lax.rsqrt(jnp.mean(x * x, axis=-1, keepdims=True) + EPS) * g


def _dot(a, b):
    return jnp.dot(a, b, preferred_element_type=F32)


def _dot_nt(a, b):
    return lax.dot_general(a, b, (((1,), (1,)), ((), ())), preferred_element_type=F32)


def _dot_tn(a, b):
    return lax.dot_general(a, b, (((0,), (0,)), ((), ())), preferred_element_type=F32)


def _strided(start, size, stride):
    return pl.ds(start, size) if stride == 1 else pl.ds(start, size, stride=stride)


def _in_proj_kernel(x_ref, g_ref, w_ref, *refs, dils, chunk):
    ng = len(dils)
    a_refs, (pb_ref, pg_ref) = refs[:ng], refs[ng:ng + 2]
    h_scr, hs_scr = refs[ng + 2], refs[ng + 3]
    perm_scr = dict(zip([d for d in dils if d > 1], refs[ng + 4:]))
    tm, dm = x_ref.shape
    h = _rms(x_ref[...], g_ref[...])
    h_scr[...] = h.astype(BF16)
    if perm_scr:
        for s in range(dm // LANES):
            hs_scr[s] = h[:, s * LANES:(s + 1) * LANES]
        for d, hp in perm_scr.items():
            n = tm // d
            for s in range(dm // LANES):
                for r in range(d):
                    hp[r * n:(r + 1) * n, s * LANES:(s + 1) * LANES] = hs_scr[s, pl.ds(r, n, stride=d), :].astype(BF16)
    base = 0
    for a_ref, d in zip(a_refs, dils):
        lhs = h_scr if d == 1 else perm_scr[d]
        n = tm // d
        for c0 in range(0, P_G, chunk):
            res = _dot(lhs[...], w_ref[:, base + c0:base + c0 + chunk]).astype(a_ref.dtype)
            if d == 1:
                a_ref[:, c0:c0 + chunk] = res
            else:
                for r in range(d):
                    a_ref[0, r, :, c0:c0 + chunk] = res[r * n:(r + 1) * n]
        base += P_G
    for o_ref in (pb_ref, pg_ref):
        width = o_ref.shape[1]
        for c0 in range(0, width, chunk):
            c1 = min(c0 + chunk, width)
            o_ref[:, c0:c1] = _dot(h_scr[...], w_ref[:, base + c0:base + c1]).astype(o_ref.dtype)
        base += width


def _in_proj(x, g, w, bsz, seq, tm, dils, out_dtype):
    m, dm = x.shape
    tps = seq // tm
    pg_w = w.shape[1] - len(dils) * P_G - P_B
    out_specs, out_shape = [], []
    for d in dils:
        if d == 1:
            out_specs.append(pl.BlockSpec((tm, P_G), lambda i: (i, 0)))
            out_shape.append(jax.ShapeDtypeStruct((m, P_G), out_dtype))
        else:
            out_specs.append(pl.BlockSpec((1, d, tm // d, P_G), lambda i: (i // tps, 0, i % tps, 0)))
            out_shape.append(jax.ShapeDtypeStruct((bsz, d, seq // d, P_G), out_dtype))
    for n in (P_B, pg_w):
        out_specs.append(pl.BlockSpec((tm, n), lambda i: (i, 0)))
        out_shape.append(jax.ShapeDtypeStruct((m, n), out_dtype))
    perms = [d for d in dils if d > 1]
    scratch = [pltpu.VMEM((tm, dm), BF16), pltpu.VMEM((dm // LANES, tm, LANES) if perms else (1, SUBLANES, LANES), F32)]
    scratch += [pltpu.VMEM((tm, dm), BF16) for _ in perms]
    return pl.pallas_call(
        functools.partial(_in_proj_kernel, dils=tuple(dils), chunk=512),
        grid=(m // tm,),
        in_specs=[pl.BlockSpec((tm, dm), lambda i: (i, 0)), _resident(g.shape), _resident(w.shape)],
        out_specs=out_specs,
        out_shape=out_shape,
        scratch_shapes=scratch,
        compiler_params=_cparams(("parallel",)),
        name="in_proj",
    )(x, g, w)


def _norm_matmul_kernel(x_ref, g_ref, w_ref, o_ref):
    o_ref[...] = _dot(_rms(x_ref[...], g_ref[...]).astype(BF16), w_ref[...]).astype(o_ref.dtype)


def _norm_matmul(x, g, w, tm):
    m, d = x.shape
    n = w.shape[1]
    return pl.pallas_call(
        _norm_matmul_kernel,
        grid=(m // tm,),
        in_specs=[pl.BlockSpec((tm, d), lambda i: (i, 0)), _resident(g.shape), _resident(w.shape)],
        out_specs=pl.BlockSpec((tm, n), lambda i: (i, 0)),
        out_shape=jax.ShapeDtypeStruct((m, n), F32),
        compiler_params=_cparams(("parallel",)),
        name="norm_matmul",
    )(x, g, w)


def _t5_bucket(n):
    max_exact = REL_BUCKETS // 2
    nf = jnp.maximum(n, 1).astype(F32)
    large = max_exact + (jnp.log(nf / max_exact) / math.log(REL_MAX_DIST / max_exact)
                         * (REL_BUCKETS - max_exact)).astype(jnp.int32)
    return jnp.where(n < max_exact, n, jnp.minimum(large, REL_BUCKETS - 1))


def _bias_by_steps(rel_bias, gi, dil):
    steps = jnp.arange(A_NK + 1, dtype=jnp.int32)
    return rel_bias[:, gi * A_HPG:(gi + 1) * A_HPG][_t5_bucket(steps * dil)].T.astype(F32)


def _dil_prompt_kernel(q_ref, kc_ref, kp_ref, vc_ref, vp_ref, u_ref, o_ref, l_ref, bias_scr, of_scr, lf_scr,
                       *, gi, dil, nsub):
    first = pl.program_id(1) == 0
    for h in range(A_HPG):
        bias_scr[h] = pltpu.roll(jnp.broadcast_to(u_ref[h:h + 1, :], (A_NK, 2 * A_NK)), 0, 1, stride=1, stride_axis=0)
    scale = A_DH ** -0.5
    col = lax.broadcasted_iota(jnp.int32, (A_NK, 2 * A_NK), 1)
    lane = lax.broadcasted_iota(jnp.int32, (A_NK, LANES), 1)
    for r in range(dil):
        for s in range(nsub):
            rows = slice(s * A_NK, (s + 1) * A_NK)
            dst = _strided(s * A_NK * dil + r, A_NK, dil)
            lt = jnp.zeros((A_NK, LANES), F32)
            for h in range(A_HPG):
                cols = slice(h * A_DH, (h + 1) * A_DH)
                if s == 0:
                    kw = jnp.concatenate([kp_ref[0, r, :, cols], kc_ref[0, r, rows, cols]], axis=0)
                    vw = jnp.concatenate([vp_ref[0, r, :, cols], vc_ref[0, r, rows, cols]], axis=0)
                else:
                    kw = kc_ref[0, r, (s - 1) * A_NK:(s + 1) * A_NK, cols]
                    vw = vc_ref[0, r, (s - 1) * A_NK:(s + 1) * A_NK, cols]
                sc = _dot_nt(q_ref[0, r, rows, cols], kw) * scale + bias_scr[h]
                if s == 0:
                    sc = jnp.where(jnp.logical_and(first, col < A_NK), NEG_INF, sc)
                mx = jnp.max(sc, axis=-1, keepdims=True)
                p = jnp.exp(sc - mx)
                den = jnp.sum(p, axis=-1, keepdims=True)
                o = _dot(p.astype(BF16), vw) / den
                if dil == 1:
                    o_ref[0, rows, cols] = o.astype(o_ref.dtype)
                else:
                    of_scr[h, dst, :] = o
                lt = jnp.where(lane == gi * A_HPG + h, mx + jnp.log(den), lt)
            if dil == 1:
                l_ref[0, rows, :] = lt
            else:
                lf_scr[dst, :] = lt
    if dil > 1:
        for h in range(A_HPG):
            o_ref[0, :, h * A_DH:(h + 1) * A_DH] = of_scr[h].astype(o_ref.dtype)
        l_ref[0] = lf_scr[...]


def _dil_prompt(a, u, gi, dil, bsz, seq, tt):
    m = seq // dil
    tt = min(tt, seq)
    mt = tt // dil
    nsub = mt // A_NK
    assert mt % A_NK == 0 and seq % tt == 0

    def cur(cb):
        return pl.BlockSpec((1, dil, mt, A_GW), lambda b, i: (b, 0, i, cb))

    def prev(cb):
        return pl.BlockSpec((1, dil, A_NK, A_GW), lambda b, i: (b, 0, jnp.maximum(i * nsub - 1, 0), cb))

    st = tt if dil > 1 else SUBLANES
    o, lse = pl.pallas_call(
        functools.partial(_dil_prompt_kernel, gi=gi, dil=dil, nsub=nsub),
        grid=(bsz, seq // tt),
        in_specs=[cur(0), cur(1), prev(1), cur(2), prev(2), _resident(u.shape)],
        out_specs=[pl.BlockSpec((1, tt, A_GW), lambda b, i: (b, i, 0)),
                   pl.BlockSpec((1, tt, LANES), lambda b, i: (b, i, 0))],
        out_shape=[jax.ShapeDtypeStruct((bsz, seq, A_GW), a.dtype), jax.ShapeDtypeStruct((bsz, seq, LANES), F32)],
        scratch_shapes=[pltpu.VMEM((A_HPG, A_NK, 2 * A_NK), F32), pltpu.VMEM((A_HPG, st, A_DH), F32),
                        pltpu.VMEM((st, LANES), F32)],
        compiler_params=_cparams(("parallel", "arbitrary")),
        name=f"dil_prompt_g{gi}",
    )(a, a, a, a, a, u)
    return o.reshape(bsz * seq, A_GW), lse.reshape(bsz * seq, LANES)


def _prompt_bias_row(rel_bias, gi, dil):
    tab = _bias_by_steps(rel_bias, gi, dil)
    return jnp.concatenate([tab[:, ::-1], jnp.full((A_HPG, A_NK - 1), NEG_INF, F32)], axis=1)


def _dil_sample_kernel(a1_ref, a2_ref, a3_ref, c1_ref, c2_ref, c3_ref, bc1_ref, bc2_ref, bc3_ref, bn_ref,
                       o1_ref, o2_ref, o3_ref, l1_ref, l2_ref, l3_ref):
    scale = A_DH ** -0.5
    nq = a1_ref.shape[1]
    groups = ((a1_ref, c1_ref, bc1_ref, o1_ref, l1_ref), (a2_ref, c2_ref, bc2_ref, o2_ref, l2_ref),
              (a3_ref, c3_ref, bc3_ref, o3_ref, l3_ref))
    for gi, (a_ref, c_ref, bc_ref, o_ref, l_ref) in enumerate(groups):
        rows = c_ref[0]
        if rows.ndim == 3:
            rows = rows.reshape(rows.shape[0] * rows.shape[1], rows.shape[2])
        rows = rows.astype(BF16)
        qs = [a_ref[0, :, h * A_DH:(h + 1) * A_DH] for h in range(A_HPG)]
        q_all = jnp.concatenate(qs, axis=0).astype(BF16)
        sc = _dot_nt(q_all, rows) * scale + bc_ref[...]
        scn = [_dot_nt(qs[h].astype(BF16), a_ref[0, :, A_GW + h * A_DH:A_GW + (h + 1) * A_DH].astype(BF16)) * scale
               + bn_ref[gi, h] for h in range(A_HPG)]
        scn = jnp.concatenate(scn, axis=0)
        mx = jnp.maximum(jnp.max(sc, axis=-1, keepdims=True), jnp.max(scn, axis=-1, keepdims=True))
        p = jnp.exp(sc - mx)
        pn = jnp.exp(scn - mx)
        den = jnp.sum(p, axis=-1, keepdims=True) + jnp.sum(pn, axis=-1, keepdims=True)
        acc = _dot(pltpu.roll(p, A_HPG, 1).astype(BF16), rows)
        lse = mx + jnp.log(den)
        lane = lax.broadcasted_iota(jnp.int32, (nq, LANES), 1)
        lt = jnp.zeros((nq, LANES), F32)
        for h in range(A_HPG):
            hr = slice(h * nq, (h + 1) * nq)
            vnew = a_ref[0, :, 2 * A_GW + h * A_DH:2 * A_GW + (h + 1) * A_DH].astype(BF16)
            o = (acc[hr] + _dot(pn[hr].astype(BF16), vnew)) / den[hr]
            o_ref[0, :, h * A_DH:(h + 1) * A_DH] = o.astype(o_ref.dtype)
            lt = jnp.where(lane == gi * A_HPG + h, lse[hr], lt)
        l_ref[0] = lt


def _sample_biases(rel_bias, nq, nres):
    cache, new = [], []
    for gi, (window, dil) in enumerate(A_GROUPS):
        tab = _bias_by_steps(rel_bias, gi, dil)
        length = window + nq
        rep = jnp.repeat(tab, dil, axis=1)
        rep = jnp.where((jnp.arange(rep.shape[1]) % dil == 0)[None], rep, NEG_INF)
        by = jnp.concatenate([rep, jnp.full((A_HPG, max(length - rep.shape[1], 0)), NEG_INF, F32)], axis=1)[:, :length]
        rev = by[:, ::-1]
        bc = jnp.stack([rev[:, nq - 1 - i:nq - 1 - i + window] for i in range(nq)], axis=1)
        bn = jnp.stack([jnp.concatenate([rev[:, length - 1 - i:], jnp.full((A_HPG, nq - 1 - i), NEG_INF, F32)], axis=1)
                        for i in range(nq)], axis=1)
        own = (jnp.arange(2 * A_HPG)[None, :] == jnp.arange(A_HPG)[:, None])
        full = jnp.where(own[:, None, None, :], bc[:, :, :, None], NEG_INF)
        if gi == len(A_GROUPS) - 1:
            full = full.reshape(A_HPG, nq, window // dil, dil, 2 * A_HPG)[:, :, :, :nres]
        cache.append(full.reshape(A_HPG * nq, -1))
        new.append(bn)
    return cache, jnp.stack(new)


def _dil_sample(a_s, caches, rel_bias):
    bsz, nq, _ = a_s[0].shape
    w3, dil3 = caches[2].shape[1], A_GROUPS[2][1]
    nres = min(dil3, nq)
    assert w3 % dil3 == 0 and nq <= dil3, "new tokens must fall in distinct residue classes of the widest dilation"
    bcs, bn = _sample_biases(rel_bias, nq, nres)
    rpp = 2 * A_HPG
    c1 = caches[0].reshape(bsz, caches[0].shape[1] * rpp, A_DH)
    c2 = caches[1].reshape(bsz, caches[1].shape[1] * rpp, A_DH)
    c3 = caches[2].reshape(bsz, w3 // dil3, dil3 * rpp, A_DH)
    row = lambda b: (b, 0, 0)
    outs = pl.pallas_call(
        _dil_sample_kernel,
        grid=(bsz,),
        in_specs=[pl.BlockSpec((1, nq, P_G), row)] * 3
        + [pl.BlockSpec((1,) + c1.shape[1:], row), pl.BlockSpec((1,) + c2.shape[1:], row),
           pl.BlockSpec((1, w3 // dil3, nres * rpp, A_DH), lambda b: (b, 0, 0, 0)),
           _resident(bcs[0].shape), _resident(bcs[1].shape), _resident(bcs[2].shape), _resident(bn.shape)],
        out_specs=[pl.BlockSpec((1, nq, A_GW), row)] * 3 + [pl.BlockSpec((1, nq, LANES), row)] * 3,
        out_shape=[jax.ShapeDtypeStruct((bsz, nq, A_GW), F32)] * 3 + [jax.ShapeDtypeStruct((bsz, nq, LANES), F32)] * 3,
        compiler_params=_cparams(("parallel",)),
        name="dil_sample",
    )(*a_s, c1, c2, c3, bcs[0], bcs[1], bcs[2], bn)
    return [o.reshape(bsz * nq, o.shape[-1]) for o in outs]


def _chunk_cumsum(x, chunk):
    pos = lax.broadcasted_iota(jnp.int32, x.shape, 0) % chunk
    sh = 1
    while sh < chunk:
        x = x + jnp.where(pos >= sh, pltpu.roll(x, sh, 0), 0.0)
        sh *= 2
    return x


def _gla_kernel(q_ref, k_ref, v_ref, gb_ref, fb_ref, wf_ref, bf_ref, gn_ref, s0_ref, o_ref, sf_ref, st_scr,
                *, chunk, nchunks):
    t = pl.program_id(1)

    @pl.when(t == 0)
    def _():
        st_scr[...] = s0_ref[0]

    tt = chunk * nchunks
    mid = chunk // 2
    z = _dot(fb_ref[0].astype(BF16), wf_ref[...]) + bf_ref[...]
    lf = (jnp.minimum(z, 0.0) - jnp.log(1.0 + jnp.exp(-jnp.abs(z)))) * (1.0 / B_GATE_NORM)
    b = _chunk_cumsum(lf, chunk)
    b3 = b.reshape(nchunks, chunk, B_QK)
    bm = jnp.broadcast_to(b3[:, mid:mid + 1, :], b3.shape).reshape(tt, B_QK)
    blr = b3[:, chunk - 1, :]
    bl = jnp.broadcast_to(b3[:, chunk - 1:chunk, :], b3.shape).reshape(tt, B_QK)
    q = q_ref[0].astype(F32) * (B_DK ** -0.5)
    k = k_ref[0].astype(F32)
    qe = (q * jnp.exp(b)).astype(BF16)
    qm = (q * jnp.exp(b - bm)).astype(BF16)
    km = (k * jnp.exp(bm - b)).astype(BF16)
    kl = (k * jnp.exp(bl - b)).astype(BF16)
    dec_t = jnp.exp(blr).T if nchunks >= SUBLANES else None
    ri = lax.broadcasted_iota(jnp.int32, (tt, tt), 0)
    ci = lax.broadcasted_iota(jnp.int32, (tt, tt), 1)
    causal = jnp.logical_and(ri >= ci, ri // chunk == ci // chunk)
    ones_col = jnp.ones((chunk, 1), F32)
    for h in range(B_HEADS):
        hs = slice(h * B_DK, (h + 1) * B_DK)
        vs = slice(h * B_DV, (h + 1) * B_DV)
        v = v_ref[0, :, vs].astype(BF16)
        a = jnp.where(causal, _dot_nt(qm[:, hs], km[:, hs]), 0.0)
        o_intra = _dot(a.astype(BF16), v)
        gbv = gb_ref[0, :, vs].astype(F32)
        gate = gbv * jax.nn.sigmoid(gbv)
        st = st_scr[h]
        for c in range(nchunks):
            rows = slice(c * chunk, (c + 1) * chunk)
            o = o_intra[rows] + _dot(qe[rows, hs], st.astype(BF16))
            if dec_t is not None:
                dcol = dec_t[hs, c:c + 1]
            else:
                dcol = jnp.exp(_dot_tn(lf[rows, hs], ones_col))
            st = st * dcol + _dot_tn(kl[rows, hs], v[rows])
            o_ref[0, rows, vs] = (_rms(o, gn_ref[...]) * gate[rows]).astype(o_ref.dtype)
        st_scr[h] = st

    @pl.when(t == pl.num_programs(1) - 1)
    def _():
        sf_ref[0] = st_scr[...]


def _gla(pb, wf, bf, gn, s0, bsz, seq, chunk, tt):
    pb3 = pb.reshape(bsz, seq, P_B)
    tt = min(tt, seq)
    o, sf = pl.pallas_call(
        functools.partial(_gla_kernel, chunk=chunk, nchunks=tt // chunk),
        grid=(bsz, seq // tt),
        in_specs=[pl.BlockSpec((1, tt, B_QK), lambda b, t: (b, t, 0)),
                  pl.BlockSpec((1, tt, B_QK), lambda b, t: (b, t, 1)),
                  pl.BlockSpec((1, tt, B_V), lambda b, t: (b, t, 2 * B_QK // B_V)),
                  pl.BlockSpec((1, tt, B_V), lambda b, t: (b, t, 2 * B_QK // B_V + 1)),
                  pl.BlockSpec((1, tt, LANES), lambda b, t: (b, t, P_FB_OFF // LANES)),
                  _resident(wf.shape), _resident(bf.shape), _resident(gn.shape),
                  pl.BlockSpec((1, B_HEADS, B_DK, B_DV), lambda b, t: (b, 0, 0, 0))],
        out_specs=[pl.BlockSpec((1, tt, B_V), lambda b, t: (b, t, 0)),
                   pl.BlockSpec((1, B_HEADS, B_DK, B_DV), lambda b, t: (b, 0, 0, 0))],
        out_shape=[jax.ShapeDtypeStruct((bsz, seq, B_V), pb.dtype),
                   jax.ShapeDtypeStruct((bsz, B_HEADS, B_DK, B_DV), F32)],
        scratch_shapes=[pltpu.VMEM((B_HEADS, B_DK, B_DV), F32)],
        compiler_params=_cparams(("parallel", "arbitrary")),
        name="gla",
    )(pb3, pb3, pb3, pb3, pb3, wf, bf, gn, s0)
    return o.reshape(bsz * seq, B_V), sf


def _mix_out_kernel(o1_ref, o2_ref, o3_ref, l1_ref, l2_ref, l3_ref, ob_ref, ga_ref, gbt_ref, x_ref,
                    ex_ref, wa_ref, wb_ref, wo_ref, gp_ref, y_ref):
    ng = len(A_GROUPS)
    ls = l1_ref[...] + l2_ref[...] + l3_ref[...]
    lane = lax.broadcasted_iota(jnp.int32, ls.shape, 1)

    def over_groups(v, op):
        red = v
        for g in range(1, ng):
            red = op(red, pltpu.roll(v, LANES - g * A_HPG, 1))
        out = red
        for g in range(1, ng):
            out = jnp.where(lane >= g * A_HPG, pltpu.roll(red, g * A_HPG, 1), out)
        return out

    e = jnp.exp(ls - over_groups(ls, jnp.maximum))
    w = jnp.where(lane < ng * A_HPG, e / over_groups(e, jnp.add), 0.0)
    w_hi = w.astype(BF16)
    w_lo = (w - w_hi.astype(F32)).astype(BF16)
    wf = _dot(w_hi, ex_ref[...]) + _dot(w_lo, ex_ref[...])
    comb = (wf[:, :A_GW] * o1_ref[...].astype(F32) + wf[:, A_GW:2 * A_GW] * o2_ref[...].astype(F32)
            + wf[:, 2 * A_GW:] * o3_ref[...].astype(F32))
    ya = _dot(comb.astype(BF16), wa_ref[...])
    yb = _dot(ob_ref[...].astype(BF16), wb_ref[...])
    mix = jax.nn.sigmoid(ga_ref[...].astype(F32)) * ya + jax.nn.sigmoid(gbt_ref[...].astype(F32)) * yb
    out = _dot(mix.astype(BF16), wo_ref[...])
    y_ref[...] = x_ref[...] + _rms(out, gp_ref[...])


def _mix_out(oas, lses, ob, gates, x, wa, wb, wo, gpost, tm):
    m, d = x.shape
    row = lambda i: (i, 0)
    ex = (jnp.arange(LANES)[:, None] == jnp.arange(A_WIDTH)[None, :] // A_DH).astype(BF16)
    return pl.pallas_call(
        _mix_out_kernel,
        grid=(m // tm,),
        in_specs=[pl.BlockSpec((tm, A_GW), row)] * 3 + [pl.BlockSpec((tm, LANES), row)] * 3
        + [pl.BlockSpec((tm, B_V), row), pl.BlockSpec((tm, d), row), pl.BlockSpec((tm, d), lambda i: (i, 1)),
           pl.BlockSpec((tm, d), row), _resident(ex.shape), _resident(wa.shape), _resident(wb.shape),
           _resident(wo.shape), _resident(gpost.shape)],
        out_specs=pl.BlockSpec((tm, d), row),
        out_shape=jax.ShapeDtypeStruct((m, d), F32),
        compiler_params=_cparams(("parallel",)),
        name="mix_out",
    )(*oas, *lses, ob, gates, gates, x, ex, wa, wb, wo, gpost)


def _cross_kernel(x_ref, kv_ref, wq_ref, wo_ref, gpre_ref, gpost_ref, y_ref, q_scr, c_scr, *, nb):
    x = x_ref[...]
    d = x.shape[1]
    dh = d // M_HEADS
    q_scr[...] = _dot(_rms(x, gpre_ref[...]).astype(BF16), wq_ref[...])
    rpb = x.shape[0] // nb
    scale = dh ** -0.5
    for j in range(nb):
        rows = slice(j * rpb, (j + 1) * rpb)
        for h in range(M_HEADS):
            cols = slice(h * dh, (h + 1) * dh)
            if len(kv_ref.shape) == 5:
                k, v = kv_ref[j, :, 0, h, :].astype(BF16), kv_ref[j, :, 1, h, :].astype(BF16)
            else:
                k = kv_ref[j, :, cols].astype(BF16)
                v = kv_ref[j, :, d + cols.start:d + cols.stop].astype(BF16)
            sc = _dot_nt(q_scr[rows, cols].astype(BF16), k) * scale
            p = jnp.exp(sc - jnp.max(sc, axis=-1, keepdims=True))
            den = jnp.sum(p, axis=-1, keepdims=True)
            c_scr[rows, cols] = _dot(p.astype(BF16), v) / den
    out = _dot(c_scr[...].astype(BF16), wo_ref[...])
    y_ref[...] = x + _rms(out, gpost_ref[...])


def _cross(x, kv, wq, wo, gpre, gpost, tm, nb, rows_per_batch):
    m, d = x.shape
    tiles_per_kv = max(rows_per_batch // tm, 1)
    kv_tail = (0,) * (kv.ndim - 1)
    return pl.pallas_call(
        functools.partial(_cross_kernel, nb=nb),
        grid=(m // tm,),
        in_specs=[pl.BlockSpec((tm, d), lambda i: (i, 0)),
                  pl.BlockSpec((nb,) + kv.shape[1:], lambda i: (i // tiles_per_kv,) + kv_tail),
                  _resident(wq.shape), _resident(wo.shape), _resident(gpre.shape), _resident(gpost.shape)],
        out_specs=pl.BlockSpec((tm, d), lambda i: (i, 0)),
        out_shape=jax.ShapeDtypeStruct((m, d), F32),
        scratch_shapes=[pltpu.VMEM((tm, d), F32), pltpu.VMEM((tm, d), F32)],
        compiler_params=_cparams(("parallel",)),
        name="cross_attn",
    )(x, kv, wq, wo, gpre, gpost)


def _ffn_kernel(x_ref, wg_ref, wu_ref, wd_ref, gpre_ref, gpost_ref, y_ref, h_scr, *, chunk):
    x = x_ref[...]
    h_scr[...] = _rms(x, gpre_ref[...]).astype(BF16)
    acc = jnp.zeros(x.shape, F32)
    for c0 in range(0, wg_ref.shape[1], chunk):
        g = _dot(h_scr[...], wg_ref[:, c0:c0 + chunk])
        u = _dot(h_scr[...], wu_ref[:, c0:c0 + chunk])
        acc = acc + _dot((g * jax.nn.sigmoid(g) * u).astype(BF16), wd_ref[c0:c0 + chunk, :])
    y_ref[...] = x + _rms(acc, gpost_ref[...])


def _ffn(x, wg, wu, wd, gpre, gpost, tm):
    m, d = x.shape
    return pl.pallas_call(
        functools.partial(_ffn_kernel, chunk=256),
        grid=(m // tm,),
        in_specs=[pl.BlockSpec((tm, d), lambda i: (i, 0)), _resident(wg.shape), _resident(wu.shape),
                  _resident(wd.shape), _resident(gpre.shape), _resident(gpost.shape)],
        out_specs=pl.BlockSpec((tm, d), lambda i: (i, 0)),
        out_shape=jax.ShapeDtypeStruct((m, d), F32),
        scratch_shapes=[pltpu.VMEM((tm, d), BF16)],
        compiler_params=_cparams(("parallel",)),
        name="ffn",
    )(x, wg, wu, wd, gpre, gpost)


def _row(v):
    return v.reshape(1, -1).astype(F32)


def _kv_rows(a, keep):
    bsz, seq, _ = a.shape
    return a[:, seq - keep:, A_GW:].astype(F32).reshape(bsz, keep, 2, A_HPG, A_DH)


def _trunk(x, mem_kv, s0, chunk, caches, rel_bias, lw, *, tm, act_dtype, cross_tm, cross_nb):
    (g_mix_pre, g_mix_post, w_in, w_f2, b_f2, gla_norm, w_pa, w_pb, w_out,
     g_mem_pre, g_mem_post, w_mq, w_mo, g_ffn_pre, g_ffn_post, w_gate, w_up, w_down) = lw
    bsz, seq, d = x.shape
    x2 = x.reshape(bsz * seq, d)
    if caches is None:
        dils = tuple(dil for _, dil in A_GROUPS)
        *a_g, pb, pg = _in_proj(x2, g_mix_pre, w_in, bsz, seq, tm, dils, act_dtype)
        oas, lses, wins = [], [], []
        for gi, (window, dil) in enumerate(A_GROUPS):
            a4 = a_g[gi].reshape(bsz, dil, seq // dil, P_G)
            o, l = _dil_prompt(a4, _prompt_bias_row(rel_bias, gi, dil), gi, dil, bsz, seq, A_NK * A_GROUPS[-1][1])
            oas.append(o)
            lses.append(l)
            keep = min(window, seq)
            tail = a4[:, :, (seq - keep) // dil:]
            wins.append(_kv_rows(tail.transpose(0, 2, 1, 3).reshape(bsz, keep, P_G), keep))
    else:
        *a_g, pb, pg = _in_proj(x2, g_mix_pre, w_in, bsz, seq, tm, (1,) * len(A_GROUPS), act_dtype)
        a_s = [a.reshape(bsz, seq, P_G) for a in a_g]
        res = _dil_sample(a_s, caches, rel_bias)
        oas, lses = res[:3], res[3:]
        wins = [_kv_rows(a, seq) for a in a_s]
    ob, s_new = _gla(pb, w_f2, b_f2, gla_norm, s0, bsz, seq, chunk, 512)
    x2 = _mix_out(oas, lses, ob, pg, x2, w_pa, w_pb, w_out, g_mix_post, tm)
    x2 = _cross(x2, mem_kv, w_mq, w_mo, g_mem_pre, g_mem_post, cross_tm, cross_nb, seq)
    x2 = _ffn(x2, w_gate, w_up, w_down, g_ffn_pre, g_ffn_post, tm)
    return x2.reshape(bsz, seq, d), wins, s_new


def kernel(x_prompt, x_sample, cache_win1_kv, cache_win2_kv, cache_win3_kv, state_gla, cache_mem_kv, mem_prompt, rel_bias, norm_mix_pre, norm_mix_post, w_in, w_f2, b_f2, gla_norm, w_proj_a, w_proj_b, w_out, norm_memtok, w_mk, w_mv, norm_mem_pre, norm_mem_post, w_mq, w_mo, norm_ffn_pre, norm_ffn_post, w_ffn_gate, w_ffn_up, w_ffn_down):
    depth = w_in.shape[0]
    bsz, seq, d = x_prompt.shape
    dbs, dseq, _ = x_sample.shape
    mem_len = mem_prompt.shape[1]
    yp, ys = x_prompt, x_sample
    outs = [[] for _ in range(9)]
    ng = len(A_GROUPS)
    for l in range(depth):
        wl = w_in[l]
        f0 = 3 * A_WIDTH + P_FB_OFF
        cols = [wl[:, part * A_WIDTH + gi * A_GW:part * A_WIDTH + (gi + 1) * A_GW] for gi in range(ng) for part in range(3)]
        cols += [wl[:, 3 * A_WIDTH:f0], jnp.pad(wl[:, f0:f0 + B_RANK], ((0, 0), (0, LANES - B_RANK))), wl[:, f0 + B_RANK:]]
        w_in_l = jnp.concatenate(cols, axis=1).astype(BF16)
        w_f2_l = jnp.pad(w_f2[l], ((0, LANES - B_RANK), (0, 0))).astype(BF16)
        lw = (_row(norm_mix_pre[l]), _row(norm_mix_post[l]), w_in_l, w_f2_l, _row(b_f2[l]), _row(gla_norm[l]),
              w_proj_a[l].astype(BF16), w_proj_b[l].astype(BF16), w_out[l].astype(BF16),
              _row(norm_mem_pre[l]), _row(norm_mem_post[l]), w_mq[l].astype(BF16), w_mo[l].astype(BF16),
              _row(norm_ffn_pre[l]), _row(norm_ffn_post[l]),
              w_ffn_gate[l].astype(BF16), w_ffn_up[l].astype(BF16), w_ffn_down[l].astype(BF16))
        w_mkv = jnp.concatenate([w_mk[l], w_mv[l]], axis=1).astype(BF16)
        mem_kv_p = _norm_matmul(mem_prompt.reshape(bsz * mem_len, d), _row(norm_memtok[l]), w_mkv,
                                min(256, bsz * mem_len)).reshape(bsz, mem_len, 2 * d)
        s0 = jnp.zeros((bsz, B_HEADS, B_DK, B_DV), F32)
        tm_p = min(512, seq)
        yp, win_p, gla_p = _trunk(yp, mem_kv_p, s0, min(B_CHUNK, seq), None, rel_bias, lw,
                                  tm=tm_p, act_dtype=BF16, cross_tm=tm_p, cross_nb=1)
        caches = (cache_win1_kv[l], cache_win2_kv[l], cache_win3_kv[l])
        cross_nb = min(4, dbs)
        ys, win_s, gla_s = _trunk(ys, cache_mem_kv[l], state_gla[l], dseq, caches,
                                  rel_bias, lw, tm=dbs * dseq, act_dtype=F32,
                                  cross_tm=cross_nb * dseq, cross_nb=cross_nb)
        for lst, v in zip(outs, (*win_p, gla_p, mem_kv_p.reshape(bsz, mem_len, 2, M_HEADS, d // M_HEADS), *win_s, gla_s)):
            lst.append(v)
    return (yp, ys, *[jnp.stack(o) for o in outs])
```

```python
import functools
import math

import numpy as np
import jax
import jax.numpy as jnp
from jax import lax
from jax.experimental import pallas as pl
from jax.experimental.pallas import tpu as pltpu

F32 = jnp.float32
BF16 = jnp.bfloat16

A_GROUPS = ((128, 1), (512, 4), (2048, 16))
A_HPG = 4
A_DH = 128
A_GW = A_HPG * A_DH
A_WIDTH = A_GW * len(A_GROUPS)
A_NK = 128
REL_BUCKETS = 32
REL_MAX_DIST = 2048
B_HEADS = 4
B_DK = 128
B_DV = 256
B_QK = B_HEADS * B_DK
B_V = B_HEADS * B_DV
B_RANK = 16
B_GATE_NORM = 16.0
B_CHUNK = 64
M_HEADS = 4
EPS = 1e-6
NEG_INF = -1e30

LANES = 128
SUBLANES = 8
VMEM_LIMIT_BYTES = 56 * 1024 * 1024

P_G = 3 * A_GW
P_B = 2 * B_QK + 2 * B_V + LANES
P_FB_OFF = 2 * B_QK + 2 * B_V


def _cparams(sem):
    return pltpu.CompilerParams(dimension_semantics=sem, vmem_limit_bytes=VMEM_LIMIT_BYTES)


def _resident(shape):
    nd = len(shape)
    return pl.BlockSpec(shape, lambda *_: (0,) * nd, pipeline_mode=pl.Buffered(1))


def _rms(x, g):
    return x * lax.rsqrt(jnp.mean(x * x, axis=-1, keepdims=True) + EPS) * g


def _dot(a, b):
    return jnp.dot(a, b, preferred_element_type=F32)


def _dot_nt(a, b):
    return lax.dot_general(a, b, (((1,), (1,)), ((), ())), preferred_element_type=F32)


def _dot_tn(a, b):
    return lax.dot_general(a, b, (((0,), (0,)), ((), ())), preferred_element_type=F32)


def _strided(start, size, stride):
    return pl.ds(start, size) if stride == 1 else pl.ds(start, size, stride=stride)


def _in_proj_kernel(x_ref, g_ref, w_ref, *refs, dils, chunk):
    ng = len(dils)
    a_refs, (pb_ref, pg_ref) = refs[:ng], refs[ng:ng + 2]
    h_scr, hs_scr = refs[ng + 2], refs[ng + 3]
    perm_scr = dict(zip([d for d in dils if d > 1], refs[ng + 4:]))
    tm, dm = x_ref.shape
    h = _rms(x_ref[...], g_ref[...])
    h_scr[...] = h.astype(BF16)
    if perm_scr:
        for s in range(dm // LANES):
            hs_scr[s] = h[:, s * LANES:(s + 1) * LANES]
        for d, hp in perm_scr.items():
            n = tm // d
            for s in range(dm // LANES):
                for r in range(d):
                    hp[r * n:(r + 1) * n, s * LANES:(s + 1) * LANES] = hs_scr[s, pl.ds(r, n, stride=d), :].astype(BF16)
    base = 0
    for a_ref, d in zip(a_refs, dils):
        lhs = h_scr if d == 1 else perm_scr[d]
        n = tm // d
        for c0 in range(0, P_G, chunk):
            res = _dot(lhs[...], w_ref[:, base + c0:base + c0 + chunk]).astype(a_ref.dtype)
            if d == 1:
                a_ref[:, c0:c0 + chunk] = res
            else:
                for r in range(d):
                    a_ref[0, r, :, c0:c0 + chunk] = res[r * n:(r + 1) * n]
        base += P_G
    for o_ref in (pb_ref, pg_ref):
        width = o_ref.shape[1]
        for c0 in range(0, width, chunk):
            c1 = min(c0 + chunk, width)
            o_ref[:, c0:c1] = _dot(h_scr[...], w_ref[:, base + c0:base + c1]).astype(o_ref.dtype)
        base += width


def _in_proj(x, g, w, bsz, seq, tm, dils, out_dtype):
    m, dm = x.shape
    tps = seq // tm
    pg_w = w.shape[1] - len(dils) * P_G - P_B
    out_specs, out_shape = [], []
    for d in dils:
        if d == 1:
            out_specs.append(pl.BlockSpec((tm, P_G), lambda i: (i, 0)))
            out_shape.append(jax.ShapeDtypeStruct((m, P_G), out_dtype))
        else:
            out_specs.append(pl.BlockSpec((1, d, tm // d, P_G), lambda i: (i // tps, 0, i % tps, 0)))
            out_shape.append(jax.ShapeDtypeStruct((bsz, d, seq // d, P_G), out_dtype))
    for n in (P_B, pg_w):
        out_specs.append(pl.BlockSpec((tm, n), lambda i: (i, 0)))
        out_shape.append(jax.ShapeDtypeStruct((m, n), out_dtype))
    perms = [d for d in dils if d > 1]
    scratch = [pltpu.VMEM((tm, dm), BF16), pltpu.VMEM((dm // LANES, tm, LANES) if perms else (1, SUBLANES, LANES), F32)]
    scratch += [pltpu.VMEM((tm, dm), BF16) for _ in perms]
    return pl.pallas_call(
        functools.partial(_in_proj_kernel, dils=tuple(dils), chunk=512),
        grid=(m // tm,),
        in_specs=[pl.BlockSpec((tm, dm), lambda i: (i, 0)), _resident(g.shape), _resident(w.shape)],
        out_specs=out_specs,
        out_shape=out_shape,
        scratch_shapes=scratch,
        compiler_params=_cparams(("parallel",)),
        name="in_proj",
    )(x, g, w)


def _norm_matmul_kernel(x_ref, g_ref, w_ref, o_ref):
    o_ref[...] = _dot(_rms(x_ref[...], g_ref[...]).astype(BF16), w_ref[...]).astype(o_ref.dtype)


def _norm_matmul(x, g, w, tm):
    m, d = x.shape
    n = w.shape[1]
    return pl.pallas_call(
        _norm_matmul_kernel,
        grid=(m // tm,),
        in_specs=[pl.BlockSpec((tm, d), lambda i: (i, 0)), _resident(g.shape), _resident(w.shape)],
        out_specs=pl.BlockSpec((tm, n), lambda i: (i, 0)),
        out_shape=jax.ShapeDtypeStruct((m, n), F32),
        compiler_params=_cparams(("parallel",)),
        name="norm_matmul",
    )(x, g, w)


def _t5_bucket(n):
    max_exact = REL_BUCKETS // 2
    nf = jnp.maximum(n, 1).astype(F32)
    large = max_exact + (jnp.log(nf / max_exact) / math.log(REL_MAX_DIST / max_exact)
                         * (REL_BUCKETS - max_exact)).astype(jnp.int32)
    return jnp.where(n < max_exact, n, jnp.minimum(large, REL_BUCKETS - 1))


def _bias_by_steps(rel_bias, gi, dil):
    steps = jnp.arange(A_NK + 1, dtype=jnp.int32)
    return rel_bias[:, gi * A_HPG:(gi + 1) * A_HPG][_t5_bucket(steps * dil)].T.astype(F32)


def _dil_prompt_kernel(q_ref, kc_ref, kp_ref, vc_ref, vp_ref, u_ref, o_ref, l_ref, bias_scr, of_scr, lf_scr,
                       *, gi, dil, nsub):
    first = pl.program_id(1) == 0
    for h in range(A_HPG):
        bias_scr[h] = pltpu.roll(jnp.broadcast_to(u_ref[h:h + 1, :], (A_NK, 2 * A_NK)), 0, 1, stride=1, stride_axis=0)
    scale = A_DH ** -0.5
    col = lax.broadcasted_iota(jnp.int32, (A_NK, 2 * A_NK), 1)
    lane = lax.broadcasted_iota(jnp.int32, (A_NK, LANES), 1)
    for r in range(dil):
        for s in range(nsub):
            rows = slice(s * A_NK, (s + 1) * A_NK)
            dst = _strided(s * A_NK * dil + r, A_NK, dil)
            lt = jnp.zeros((A_NK, LANES), F32)
            for h in range(A_HPG):
                cols = slice(h * A_DH, (h + 1) * A_DH)
                if s == 0:
                    kw = jnp.concatenate([kp_ref[0, r, :, cols], kc_ref[0, r, rows, cols]], axis=0)
                    vw = jnp.concatenate([vp_ref[0, r, :, cols], vc_ref[0, r, rows, cols]], axis=0)
                else:
                    kw = kc_ref[0, r, (s - 1) * A_NK:(s + 1) * A_NK, cols]
                    vw = vc_ref[0, r, (s - 1) * A_NK:(s + 1) * A_NK, cols]
                sc = _dot_nt(q_ref[0, r, rows, cols], kw) * scale + bias_scr[h]
                if s == 0:
                    sc = jnp.where(jnp.logical_and(first, col < A_NK), NEG_INF, sc)
                mx = jnp.max(sc, axis=-1, keepdims=True)
                p = jnp.exp(sc - mx)
                den = jnp.sum(p, axis=-1, keepdims=True)
                o = _dot(p.astype(BF16), vw) / den
                if dil == 1:
                    o_ref[0, rows, cols] = o.astype(o_ref.dtype)
                else:
                    of_scr[h, dst, :] = o
                lt = jnp.where(lane == gi * A_HPG + h, mx + jnp.log(den), lt)
            if dil == 1:
                l_ref[0, rows, :] = lt
            else:
                lf_scr[dst, :] = lt
    if dil > 1:
        for h in range(A_HPG):
            o_ref[0, :, h * A_DH:(h + 1) * A_DH] = of_scr[h].astype(o_ref.dtype)
        l_ref[0] = lf_scr[...]


def _dil_prompt(a, u, gi, dil, bsz, seq, tt):
    m = seq // dil
    tt = min(tt, seq)
    mt = tt // dil
    nsub = mt // A_NK
    assert mt % A_NK == 0 and seq % tt == 0

    def cur(cb):
        return pl.BlockSpec((1, dil, mt, A_GW), lambda b, i: (b, 0, i, cb))

    def prev(cb):
        return pl.BlockSpec((1, dil, A_NK, A_GW), lambda b, i: (b, 0, jnp.maximum(i * nsub - 1, 0), cb))

    st = tt if dil > 1 else SUBLANES
    o, lse = pl.pallas_call(
        functools.partial(_dil_prompt_kernel, gi=gi, dil=dil, nsub=nsub),
        grid=(bsz, seq // tt),
        in_specs=[cur(0), cur(1), prev(1), cur(2), prev(2), _resident(u.shape)],
        out_specs=[pl.BlockSpec((1, tt, A_GW), lambda b, i: (b, i, 0)),
                   pl.BlockSpec((1, tt, LANES), lambda b, i: (b, i, 0))],
        out_shape=[jax.ShapeDtypeStruct((bsz, seq, A_GW), a.dtype), jax.ShapeDtypeStruct((bsz, seq, LANES), F32)],
        scratch_shapes=[pltpu.VMEM((A_HPG, A_NK, 2 * A_NK), F32), pltpu.VMEM((A_HPG, st, A_DH), F32),
                        pltpu.VMEM((st, LANES), F32)],
        compiler_params=_cparams(("parallel", "arbitrary")),
        name=f"dil_prompt_g{gi}",
    )(a, a, a, a, a, u)
    return o.reshape(bsz * seq, A_GW), lse.reshape(bsz * seq, LANES)


def _prompt_bias_row(rel_bias, gi, dil):
    tab = _bias_by_steps(rel_bias, gi, dil)
    return jnp.concatenate([tab[:, ::-1], jnp.full((A_HPG, A_NK - 1), NEG_INF, F32)], axis=1)


def _dil_sample_kernel(a1_ref, a2_ref, a3_ref, c1_ref, c2_ref, c3_ref, bc1_ref, bc2_ref, bc3_ref, bn_ref,
                       o1_ref, o2_ref, o3_ref, l1_ref, l2_ref, l3_ref):
    scale = A_DH ** -0.5
    nq = a1_ref.shape[1]
    groups = ((a1_ref, c1_ref, bc1_ref, o1_ref, l1_ref), (a2_ref, c2_ref, bc2_ref, o2_ref, l2_ref),
              (a3_ref, c3_ref, bc3_ref, o3_ref, l3_ref))
    for gi, (a_ref, c_ref, bc_ref, o_ref, l_ref) in enumerate(groups):
        rows = c_ref[0]
        if rows.ndim == 3:
            rows = rows.reshape(rows.shape[0] * rows.shape[1], rows.shape[2])
        rows = rows.astype(BF16)
        qs = [a_ref[0, :, h * A_DH:(h + 1) * A_DH] for h in range(A_HPG)]
        q_all = jnp.concatenate(qs, axis=0).astype(BF16)
        sc = _dot_nt(q_all, rows) * scale + bc_ref[...]
        scn = [_dot_nt(qs[h].astype(BF16), a_ref[0, :, A_GW + h * A_DH:A_GW + (h + 1) * A_DH].astype(BF16)) * scale
               + bn_ref[gi, h] for h in range(A_HPG)]
        scn = jnp.concatenate(scn, axis=0)
        mx = jnp.maximum(jnp.max(sc, axis=-1, keepdims=True), jnp.max(scn, axis=-1, keepdims=True))
        p = jnp.exp(sc - mx)
        pn = jnp.exp(scn - mx)
        den = jnp.sum(p, axis=-1, keepdims=True) + jnp.sum(pn, axis=-1, keepdims=True)
        acc = _dot(pltpu.roll(p, A_HPG, 1).astype(BF16), rows)
        lse = mx + jnp.log(den)
        lane = lax.broadcasted_iota(jnp.int32, (nq, LANES), 1)
        lt = jnp.zeros((nq, LANES), F32)
        for h in range(A_HPG):
            hr = slice(h * nq, (h + 1) * nq)
            vnew = a_ref[0, :, 2 * A_GW + h * A_DH:2 * A_GW + (h + 1) * A_DH].astype(BF16)
            o = (acc[hr] + _dot(pn[hr].astype(BF16), vnew)) / den[hr]
            o_ref[0, :, h * A_DH:(h + 1) * A_DH] = o.astype(o_ref.dtype)
            lt = jnp.where(lane == gi * A_HPG + h, lse[hr], lt)
        l_ref[0] = lt


def _toeplitz(v, n, width):
    lead, length = v.shape[:-1], v.shape[-1]
    t = jnp.tile(v, (1,) * len(lead) + (n,))[..., :n * (length - 1)]
    return t.reshape(lead + (n, length - 1))[..., :width]


def _sample_biases(rel_bias, nq, nres):
    cache, new = [], []
    for gi, (window, dil) in enumerate(A_GROUPS):
        tab = _bias_by_steps(rel_bias, gi, dil)
        length = window + nq
        rep = jnp.repeat(tab, dil, axis=1)
        rep = jnp.where((jnp.arange(rep.shape[1]) % dil == 0)[None], rep, NEG_INF)
        by = jnp.concatenate([rep, jnp.full((A_HPG, max(length - rep.shape[1], 0)), NEG_INF, F32)], axis=1)[:, :length]
        bc = _toeplitz(jnp.roll(by[:, ::-1], -(nq - 1), axis=1), nq, window)
        bn = _toeplitz(jnp.concatenate([by[:, :1], jnp.full((A_HPG, nq), NEG_INF, F32), by[:, nq - 1:0:-1]], axis=1),
                       nq, nq)
        own = (jnp.arange(2 * A_HPG)[None, :] == jnp.arange(A_HPG)[:, None])
        full = jnp.where(own[:, None, None, :], bc[:, :, :, None], NEG_INF)
        if gi == len(A_GROUPS) - 1:
            full = full.reshape(A_HPG, nq, window // dil, dil, 2 * A_HPG)[:, :, :, :nres]
        cache.append(full.reshape(A_HPG * nq, -1))
        new.append(bn)
    return cache, jnp.stack(new)


def _dil_sample(a_s, caches, rel_bias):
    bsz, nq, _ = a_s[0].shape
    w3, dil3 = caches[2].shape[1], A_GROUPS[2][1]
    nres = min(dil3, nq)
    assert w3 % dil3 == 0 and nq <= dil3, "new tokens must fall in distinct residue classes of the widest dilation"
    bcs, bn = _sample_biases(rel_bias, nq, nres)
    rpp = 2 * A_HPG
    c1 = caches[0].reshape(bsz, caches[0].shape[1] * rpp, A_DH)
    c2 = caches[1].reshape(bsz, caches[1].shape[1] * rpp, A_DH)
    c3 = caches[2].reshape(bsz, w3 // dil3, dil3 * rpp, A_DH)
    row = lambda b: (b, 0, 0)
    outs = pl.pallas_call(
        _dil_sample_kernel,
        grid=(bsz,),
        in_specs=[pl.BlockSpec((1, nq, P_G), row)] * 3
        + [pl.BlockSpec((1,) + c1.shape[1:], row), pl.BlockSpec((1,) + c2.shape[1:], row),
           pl.BlockSpec((1, w3 // dil3, nres * rpp, A_DH), lambda b: (b, 0, 0, 0)),
           _resident(bcs[0].shape), _resident(bcs[1].shape), _resident(bcs[2].shape), _resident(bn.shape)],
        out_specs=[pl.BlockSpec((1, nq, A_GW), row)] * 3 + [pl.BlockSpec((1, nq, LANES), row)] * 3,
        out_shape=[jax.ShapeDtypeStruct((bsz, nq, A_GW), F32)] * 3 + [jax.ShapeDtypeStruct((bsz, nq, LANES), F32)] * 3,
        compiler_params=_cparams(("parallel",)),
        name="dil_sample",
    )(*a_s, c1, c2, c3, bcs[0], bcs[1], bcs[2], bn)
    return [o.reshape(bsz * nq, o.shape[-1]) for o in outs]


def _gla_kernel(q_ref, k_ref, v_ref, gb_ref, fb_ref, wf_ref, bf_ref, gn_ref, s0_ref, o_ref, sf_ref, st_scr,
                *, chunk, nchunks):
    t = pl.program_id(1)

    @pl.when(t == 0)
    def _():
        st_scr[...] = s0_ref[0]

    tt = chunk * nchunks
    mid = chunk // 2
    z = _dot(fb_ref[0].astype(BF16), wf_ref[...]) + bf_ref[...]
    lf_all = (jnp.minimum(z, 0.0) - jnp.log(1.0 + jnp.exp(-jnp.abs(z)))) * (1.0 / B_GATE_NORM)
    blk = tt
    ri = lax.broadcasted_iota(jnp.int32, (blk, blk), 0)
    ci = lax.broadcasted_iota(jnp.int32, (blk, blk), 1)
    causal = jnp.logical_and(ri >= ci, ri // chunk == ci // chunk)
    pos = lax.broadcasted_iota(jnp.int32, (tt, B_DK), 0) % chunk
    ones_col = jnp.ones((chunk, 1), F32)
    hs = slice(0, B_DK)
    for h in range(B_HEADS):
        hcols = slice(h * B_DK, (h + 1) * B_DK)
        vs = slice(h * B_DV, (h + 1) * B_DV)
        lf = lf_all[:, hcols]
        b = lf
        sh = 1
        while sh < chunk:
            b = b + jnp.where(pos >= sh, pltpu.roll(b, sh, 0), 0.0)
            sh *= 2
        b3 =b.reshape(nchunks, chunk, B_DK)
        bm = jnp.broadcast_to(b3[:, mid:mid + 1, :], b3.shape).reshape(tt, B_DK)
        blr = b3[:, chunk - 1, :]
        bl = jnp.broadcast_to(b3[:, chunk - 1:chunk, :], b3.shape).reshape(tt, B_DK)
        q = q_ref[0, :, hcols].astype(F32) * (B_DK ** -0.5)
        k = k_ref[0, :, hcols].astype(F32)
        qe = (q * jnp.exp(b)).astype(BF16)
        qm = (q * jnp.exp(b - bm)).astype(BF16)
        km = (k * jnp.exp(bm - b)).astype(BF16)
        kl = (k * jnp.exp(bl - b)).astype(BF16)
        dec_t = jnp.exp(blr).T if nchunks >= SUBLANES else None
        v = v_ref[0, :, vs].astype(BF16)
        chunks = [slice(c * chunk, (c + 1) * chunk) for c in range(nchunks)]
        us = [_dot_tn(kl[rows, hs], v[rows]) for rows in chunks]
        a = jnp.where(causal, _dot_nt(qm[:, hs], km[:, hs]), 0.0)
        o_intra = _dot(a.astype(BF16), v)
        gbv = gb_ref[0, :, vs].astype(F32)
        gate = gbv * jax.nn.sigmoid(gbv)
        st = st_scr[h]
        sts = []
        for c, rows in enumerate(chunks):
            sts.append(st.astype(BF16))
            if dec_t is not None:
                dcol = dec_t[hs, c:c + 1]
            else:
                dcol = jnp.exp(_dot_tn(lf[rows, hs], ones_col))
            st = st * dcol + us[c]
        for c, rows in enumerate(chunks):
            o = o_intra[rows] + _dot(qe[rows, hs], sts[c])
            o_ref[0, rows, vs] = (_rms(o, gn_ref[...]) * gate[rows]).astype(o_ref.dtype)
        st_scr[h] = st

    @pl.when(t == pl.num_programs(1) - 1)
    def _():
        sf_ref[0] = st_scr[...]


def _gla(pb, wf, bf, gn, s0, bsz, seq, chunk, tt):
    pb3 = pb.reshape(bsz, seq, P_B)
    tt = min(tt, seq)
    o, sf = pl.pallas_call(
        functools.partial(_gla_kernel, chunk=chunk, nchunks=tt // chunk),
        grid=(bsz, seq // tt),
        in_specs=[pl.BlockSpec((1, tt, B_QK), lambda b, t: (b, t, 0)),
                  pl.BlockSpec((1, tt, B_QK), lambda b, t: (b, t, 1)),
                  pl.BlockSpec((1, tt, B_V), lambda b, t: (b, t, 2 * B_QK // B_V)),
                  pl.BlockSpec((1, tt, B_V), lambda b, t: (b, t, 2 * B_QK // B_V + 1)),
                  pl.BlockSpec((1, tt, LANES), lambda b, t: (b, t, P_FB_OFF // LANES)),
                  _resident(wf.shape), _resident(bf.shape), _resident(gn.shape),
                  pl.BlockSpec((1, B_HEADS, B_DK, B_DV), lambda b, t: (b, 0, 0, 0))],
        out_specs=[pl.BlockSpec((1, tt, B_V), lambda b, t: (b, t, 0)),
                   pl.BlockSpec((1, B_HEADS, B_DK, B_DV), lambda b, t: (b, 0, 0, 0))],
        out_shape=[jax.ShapeDtypeStruct((bsz, seq, B_V), pb.dtype),
                   jax.ShapeDtypeStruct((bsz, B_HEADS, B_DK, B_DV), F32)],
        scratch_shapes=[pltpu.VMEM((B_HEADS, B_DK, B_DV), F32)],
        compiler_params=_cparams(("parallel", "arbitrary")),
        name="gla",
    )(pb3, pb3, pb3, pb3, pb3, wf, bf, gn, s0)
    return o.reshape(bsz * seq, B_V), sf


def _mix_out_kernel(o1_ref, o2_ref, o3_ref, l1_ref, l2_ref, l3_ref, ob_ref, ga_ref, gbt_ref, x_ref,
                    ex_ref, wa_ref, wb_ref, wo_ref, gp_ref, y_ref, *, nsplit):
    n = x_ref.shape[0] // nsplit
    for i in range(nsplit):
        rows = slice(i * n, (i + 1) * n)
        _mix_out_rows(*(r.at[rows] for r in (o1_ref, o2_ref, o3_ref, l1_ref, l2_ref, l3_ref, ob_ref, ga_ref, gbt_ref,
                                             x_ref)), ex_ref, wa_ref, wb_ref, wo_ref, gp_ref, y_ref.at[rows])


def _mix_out_rows(o1_ref, o2_ref, o3_ref, l1_ref, l2_ref, l3_ref, ob_ref, ga_ref, gbt_ref, x_ref,
                  ex_ref, wa_ref, wb_ref, wo_ref, gp_ref, y_ref):
    yb = _dot(ob_ref[...].astype(BF16), wb_ref[...])
    ng = len(A_GROUPS)
    ls = l1_ref[...] + l2_ref[...] + l3_ref[...]
    lane = lax.broadcasted_iota(jnp.int32, ls.shape, 1)

    def over_groups(v, op):
        red = v
        for g in range(1, ng):
            red = op(red, pltpu.roll(v, LANES - g * A_HPG, 1))
        out = red
        for g in range(1, ng):
            out = jnp.where(lane >= g * A_HPG, pltpu.roll(red, g * A_HPG, 1), out)
        return out

    e = jnp.exp(ls - over_groups(ls, jnp.maximum))
    w = jnp.where(lane < ng * A_HPG, e / over_groups(e, jnp.add), 0.0)
    w_hi = w.astype(BF16)
    w_lo = (w - w_hi.astype(F32)).astype(BF16)
    wf = _dot(w_hi, ex_ref[...]) + _dot(w_lo, ex_ref[...])
    comb = (wf[:, :A_GW] * o1_ref[...].astype(F32) + wf[:, A_GW:2 * A_GW] * o2_ref[...].astype(F32)
            + wf[:, 2 * A_GW:] * o3_ref[...].astype(F32))
    ya = _dot(comb.astype(BF16), wa_ref[...])
    mix =jax.nn.sigmoid(ga_ref[...].astype(F32)) * ya + jax.nn.sigmoid(gbt_ref[...].astype(F32)) * yb
    out = _dot(mix.astype(BF16), wo_ref[...])
    y_ref[...] = x_ref[...] + _rms(out, gp_ref[...])


def _mix_out(oas, lses, ob, gates, x, wa, wb, wo, gpost, tm):
    m, d = x.shape
    row = lambda i: (i, 0)
    ex = (jnp.arange(LANES)[:, None] == jnp.arange(A_WIDTH)[None, :] // A_DH).astype(BF16)
    return pl.pallas_call(
        functools.partial(_mix_out_kernel, nsplit=2 if tm % (2 * 2 * SUBLANES) == 0 else 1),
        grid=(m // tm,),
        in_specs=[pl.BlockSpec((tm, A_GW), row)] * 3 + [pl.BlockSpec((tm, LANES), row)] * 3
        + [pl.BlockSpec((tm, B_V), row), pl.BlockSpec((tm, d), row), pl.BlockSpec((tm, d), lambda i: (i, 1)),
           pl.BlockSpec((tm, d), row), _resident(ex.shape), _resident(wa.shape), _resident(wb.shape),
           _resident(wo.shape), _resident(gpost.shape)],
        out_specs=pl.BlockSpec((tm, d), row),
        out_shape=jax.ShapeDtypeStruct((m, d), F32),
        compiler_params=_cparams(("parallel",)),
        name="mix_out",
    )(*oas, *lses, ob, gates, gates, x, ex, wa, wb, wo, gpost)


def _cross_kernel(x_ref, kv_ref, wq_ref, wo_ref, gpre_ref, gpost_ref, y_ref, q_scr, c_scr, *, nb, nsplit):
    tm = x_ref.shape[0]
    if nb == 1:
        n = tm // nsplit
        for i in range(nsplit):
            _cross_rows(x_ref, kv_ref, wq_ref, wo_ref, gpre_ref, gpost_ref, y_ref, q_scr, c_scr,
                        slice(i * n, (i + 1) * n), [(0, slice(i * n, (i + 1) * n))])
    else:
        rpb = tm // nb
        _cross_rows(x_ref, kv_ref, wq_ref, wo_ref, gpre_ref, gpost_ref, y_ref, q_scr, c_scr,
                    slice(0, tm), [(j, slice(j * rpb, (j + 1) * rpb)) for j in range(nb)])


def _cross_rows(x_ref, kv_ref, wq_ref, wo_ref, gpre_ref, gpost_ref, y_ref, q_scr, c_scr, tile, batches):
    x = x_ref[tile]
    d = x.shape[1]
    dh = d // M_HEADS
    q_scr[tile] = _dot(_rms(x, gpre_ref[...]).astype(BF16), wq_ref[...])
    scale = dh ** -0.5
    for j, rows in batches:
        rpb = rows.stop - rows.start
        if len(kv_ref.shape) == 5:
            mem = kv_ref.shape[1]
            kall = kv_ref[j, :, 0].reshape(mem * M_HEADS, dh).astype(BF16)
            vall = kv_ref[j, :, 1].reshape(mem * M_HEADS, dh).astype(BF16)
            qall = jnp.concatenate([q_scr[rows, h * dh:(h + 1) * dh] for h in range(M_HEADS)], axis=0).astype(BF16)
            rh = lax.broadcasted_iota(jnp.int32, (M_HEADS * rpb, mem * M_HEADS), 0) // rpb
            ch = lax.broadcasted_iota(jnp.int32, (M_HEADS * rpb, mem * M_HEADS), 1) % M_HEADS
            sc = jnp.where(rh == ch, _dot_nt(qall, kall) * scale, NEG_INF)
            p = jnp.exp(sc - jnp.max(sc, axis=-1, keepdims=True))
            ctx = _dot(p.astype(BF16), vall) / jnp.sum(p, axis=-1, keepdims=True)
            for h in range(M_HEADS):
                c_scr[rows, h * dh:(h + 1) * dh] = ctx[h * rpb:(h + 1) * rpb]
            continue
        for h in range(M_HEADS):
            cols = slice(h * dh, (h + 1) * dh)
            k = kv_ref[j, :, cols].astype(BF16)
            v = kv_ref[j, :, d + cols.start:d + cols.stop].astype(BF16)
            sc = _dot_nt(q_scr[rows, cols].astype(BF16), k) * scale
            p = jnp.exp(sc - jnp.max(sc, axis=-1, keepdims=True))
            den = jnp.sum(p, axis=-1, keepdims=True)
            c_scr[rows, cols] = _dot(p.astype(BF16), v) / den
    out = _dot(c_scr[tile].astype(BF16), wo_ref[...])
    y_ref[tile] = x + _rms(out, gpost_ref[...])


def _cross(x, kv, wq, wo, gpre, gpost, tm, nb, rows_per_batch):
    m, d = x.shape
    tiles_per_kv = max(rows_per_batch // tm, 1)
    kv_tail = (0,) * (kv.ndim - 1)
    return pl.pallas_call(
        functools.partial(_cross_kernel, nb=nb, nsplit=1),
        grid=(m // tm,),
        in_specs=[pl.BlockSpec((tm, d), lambda i: (i, 0)),
                  pl.BlockSpec((nb,) + kv.shape[1:], lambda i: (i // tiles_per_kv,) + kv_tail),
                  _resident(wq.shape), _resident(wo.shape), _resident(gpre.shape), _resident(gpost.shape)],
        out_specs=pl.BlockSpec((tm, d), lambda i: (i, 0)),
        out_shape=jax.ShapeDtypeStruct((m, d), F32),
        scratch_shapes=[pltpu.VMEM((tm, d), F32), pltpu.VMEM((tm, d), F32)],
        compiler_params=_cparams(("parallel",)),
        name="cross_attn",
    )(x, kv, wq, wo, gpre, gpost)


def _ffn_kernel(x_ref, wg_ref, wu_ref, wd_ref, gpre_ref, gpost_ref, y_ref, h_scr, *, chunk, nsplit):
    n = x_ref.shape[0] // nsplit
    for i in range(nsplit):
        rows = slice(i * n, (i + 1) * n)
        x = x_ref[rows]
        h_scr[rows] = _rms(x, gpre_ref[...]).astype(BF16)
        acc = jnp.zeros(x.shape, F32)
        for c0 in range(0, wg_ref.shape[1], chunk):
            g = _dot(h_scr[rows], wg_ref[:, c0:c0 + chunk])
            u = _dot(h_scr[rows], wu_ref[:, c0:c0 + chunk])
            acc = acc + _dot((g * jax.nn.sigmoid(g) * u).astype(BF16), wd_ref[c0:c0 + chunk, :])
        y_ref[rows] = x + _rms(acc, gpost_ref[...])


def _ffn(x, wg, wu, wd, gpre, gpost, tm):
    m, d = x.shape
    return pl.pallas_call(
        functools.partial(_ffn_kernel, chunk=256, nsplit=1),
        grid=(m // tm,),
        in_specs=[pl.BlockSpec((tm, d), lambda i: (i, 0)), _resident(wg.shape), _resident(wu.shape),
                  _resident(wd.shape), _resident(gpre.shape), _resident(gpost.shape)],
        out_specs=pl.BlockSpec((tm, d), lambda i: (i, 0)),
        out_shape=jax.ShapeDtypeStruct((m, d), F32),
        scratch_shapes=[pltpu.VMEM((tm, d), BF16)],
        compiler_params=_cparams(("parallel",)),
        name="ffn",
    )(x, wg, wu, wd, gpre, gpost)


def _row(v):
    return v.reshape(1, -1).astype(F32)


def _kv_rows(a, keep):
    bsz, seq, _ = a.shape
    return a[:, seq - keep:, A_GW:].astype(F32).reshape(bsz, keep, 2, A_HPG, A_DH)


def _trunk(x, mem_kv, s0, chunk, caches, rel_bias, lw, *, tm, act_dtype, cross_tm, cross_nb):
    (g_mix_pre, g_mix_post, w_in, w_f2, b_f2, gla_norm, w_pa, w_pb, w_out,
     g_mem_pre, g_mem_post, w_mq, w_mo, g_ffn_pre, g_ffn_post, w_gate, w_up, w_down) = lw
    bsz, seq, d = x.shape
    x2 = x.reshape(bsz * seq, d)
    if caches is None:
        dils = tuple(dil for _, dil in A_GROUPS)
        *a_g, pb, pg = _in_proj(x2, g_mix_pre, w_in, bsz, seq, tm, dils, act_dtype)
        oas, lses, wins = [], [], []
        for gi, (window, dil) in enumerate(A_GROUPS):
            a4 = a_g[gi].reshape(bsz, dil, seq // dil, P_G)
            o, l = _dil_prompt(a4, _prompt_bias_row(rel_bias, gi, dil), gi, dil, bsz, seq, A_NK * A_GROUPS[-1][1])
            oas.append(o)
            lses.append(l)
            keep = min(window, seq)
            tail = a4[:, :, (seq - keep) // dil:]
            wins.append(_kv_rows(tail.transpose(0, 2, 1, 3).reshape(bsz, keep, P_G), keep))
    else:
        *a_g, pb, pg = _in_proj(x2, g_mix_pre, w_in, bsz, seq, tm, (1,) * len(A_GROUPS), act_dtype)
        a_s = [a.reshape(bsz, seq, P_G) for a in a_g]
        res = _dil_sample(a_s, caches, rel_bias)
        oas, lses = res[:3], res[3:]
        wins = [_kv_rows(a, seq) for a in a_s]
    ob, s_new = _gla(pb, w_f2, b_f2, gla_norm, s0, bsz, seq, chunk, 512)
    x2 = _mix_out(oas, lses, ob, pg, x2, w_pa, w_pb, w_out, g_mix_post, tm)
    x2 = _cross(x2, mem_kv, w_mq, w_mo, g_mem_pre, g_mem_post, cross_tm, cross_nb, seq)
    x2 = _ffn(x2, w_gate, w_up, w_down, g_ffn_pre, g_ffn_post, min(2 * tm, bsz * seq))
    return x2.reshape(bsz, seq, d), wins, s_new


def kernel(x_prompt, x_sample, cache_win1_kv, cache_win2_kv, cache_win3_kv, state_gla, cache_mem_kv, mem_prompt, rel_bias, norm_mix_pre, norm_mix_post, w_in, w_f2, b_f2, gla_norm, w_proj_a, w_proj_b, w_out, norm_memtok, w_mk, w_mv, norm_mem_pre, norm_mem_post, w_mq, w_mo, norm_ffn_pre, norm_ffn_post, w_ffn_gate, w_ffn_up, w_ffn_down):
    depth = w_in.shape[0]
    bsz, seq, d = x_prompt.shape
    dbs, dseq, _ = x_sample.shape
    mem_len = mem_prompt.shape[1]
    yp, ys = x_prompt, x_sample
    outs = [[] for _ in range(9)]
    ng = len(A_GROUPS)
    for l in range(depth):
        wl = w_in[l]
        f0 = 3 * A_WIDTH + P_FB_OFF
        cols = [wl[:, part * A_WIDTH + gi * A_GW:part * A_WIDTH + (gi + 1) * A_GW] for gi in range(ng) for part in range(3)]
        cols += [wl[:, 3 * A_WIDTH:f0], jnp.pad(wl[:, f0:f0 + B_RANK], ((0, 0), (0, LANES - B_RANK))), wl[:, f0 + B_RANK:]]
        w_in_l = jnp.concatenate(cols, axis=1).astype(BF16)
        w_f2_l = jnp.pad(w_f2[l], ((0, LANES - B_RANK), (0, 0))).astype(BF16)
        lw = (_row(norm_mix_pre[l]), _row(norm_mix_post[l]), w_in_l, w_f2_l, _row(b_f2[l]), _row(gla_norm[l]),
              w_proj_a[l].astype(BF16), w_proj_b[l].astype(BF16), w_out[l].astype(BF16),
              _row(norm_mem_pre[l]), _row(norm_mem_post[l]), w_mq[l].astype(BF16), w_mo[l].astype(BF16),
              _row(norm_ffn_pre[l]), _row(norm_ffn_post[l]),
              w_ffn_gate[l].astype(BF16), w_ffn_up[l].astype(BF16), w_ffn_down[l].astype(BF16))
        w_mkv = jnp.concatenate([w_mk[l], w_mv[l]], axis=1).astype(BF16)
        mem_kv_p = _norm_matmul(mem_prompt.reshape(bsz * mem_len, d), _row(norm_memtok[l]), w_mkv,
                                min(256, bsz * mem_len)).reshape(bsz, mem_len, 2 * d)
        s0 = jnp.zeros((bsz, B_HEADS, B_DK, B_DV), F32)
        tm_p = min(512, seq)
        yp, win_p, gla_p = _trunk(yp, mem_kv_p, s0, min(B_CHUNK, seq), None, rel_bias, lw,
                                  tm=tm_p, act_dtype=BF16, cross_tm=tm_p, cross_nb=1)
        caches = (cache_win1_kv[l], cache_win2_kv[l], cache_win3_kv[l])
        cross_nb = min(4, dbs)
        ys, win_s, gla_s = _trunk(ys, cache_mem_kv[l], state_gla[l], dseq, caches,
                                  rel_bias, lw, tm=dbs * dseq, act_dtype=F32,
                                  cross_tm=cross_nb * dseq, cross_nb=cross_nb)
        for lst, v in zip(outs, (*win_p, gla_p, mem_kv_p.reshape(bsz, mem_len, 2, M_HEADS, d // M_HEADS), *win_s, gla_s)):
            lst.append(v)
    return (yp, ys, *[jnp.stack(o) for o in outs])
```

```python
import functools
import math

import numpy as np
import jax
import jax.numpy as jnp
from jax import lax
from jax.experimental import pallas as pl
from jax.experimental.pallas import tpu as pltpu

F32 = jnp.float32
BF16 = jnp.bfloat16

A_GROUPS = ((128, 1), (512, 4), (2048, 16))
A_HPG = 4
A_DH = 128
A_GW = A_HPG * A_DH
A_WIDTH = A_GW * len(A_GROUPS)
A_NK = 128
REL_BUCKETS = 32
REL_MAX_DIST = 2048
B_HEADS = 4
B_DK = 128
B_DV = 256
B_QK = B_HEADS * B_DK
B_V = B_HEADS * B_DV
B_RANK = 16
B_GATE_NORM = 16.0
B_CHUNK = 64
M_HEADS = 4
EPS = 1e-6
NEG_INF = -1e30

LANES = 128
SUBLANES = 8
VMEM_LIMIT_BYTES = 56 * 1024 * 1024

P_G = 3 * A_GW
P_B = 2 * B_QK + 2 * B_V + LANES
P_FB_OFF = 2 * B_QK + 2 * B_V


def _cparams(sem):
    return pltpu.CompilerParams(dimension_semantics=sem, vmem_limit_bytes=VMEM_LIMIT_BYTES)


def _resident(shape):
    nd = len(shape)
    return pl.BlockSpec(shape, lambda *_: (0,) * nd, pipeline_mode=pl.Buffered(1))


def _rms(x, g):
    return x * lax.rsqrt(jnp.mean(x * x, axis=-1, keepdims=True) + EPS) * g


def _dot(a, b):
    return jnp.dot(a, b, preferred_element_type=F32)


def _dot_nt(a, b):
    return lax.dot_general(a, b, (((1,), (1,)), ((), ())), preferred_element_type=F32)


def _dot_tn(a, b):
    return lax.dot_general(a, b, (((0,), (0,)), ((), ())), preferred_element_type=F32)


def _strided(start, size, stride):
    return pl.ds(start, size) if stride == 1 else pl.ds(start, size, stride=stride)


def _in_proj_kernel(x_ref, g_ref, w_ref, wfb_ref, wgt_ref, *refs, dils, chunk):
    ng = len(dils)
    a_refs, (pb_ref, pg_ref) = refs[:ng], refs[ng:ng + 2]
    h_scr, hs_scr = refs[ng + 2], refs[ng + 3]
    perm_scr = dict(zip([d for d in dils if d > 1], refs[ng + 4:]))
    tm, dm = x_ref.shape
    h = _rms(x_ref[...], g_ref[...])
    h_scr[...] = h.astype(BF16)
    if perm_scr:
        for s in range(dm // LANES):
            hs_scr[s] = h[:, s * LANES:(s + 1) * LANES]
        for d, hp in perm_scr.items():
            n = tm // d
            for s in range(dm // LANES):
                for r in range(d):
                    hp[r * n:(r + 1) * n, s * LANES:(s + 1) * LANES] = hs_scr[s, pl.ds(r, n, stride=d), :].astype(BF16)
    for gi, (a_ref, d) in enumerate(zip(a_refs, dils)):
        lhs = h_scr if d == 1 else perm_scr[d]
        n = tm // d
        for part in range(3):
            src = part * A_WIDTH + gi * A_GW
            res = _dot(lhs[...], w_ref[:, src:src + A_GW]).astype(a_ref.dtype)
            if d == 1:
                a_ref[:, part * A_GW:(part + 1) * A_GW] = res
            else:
                for r in range(d):
                    a_ref[0, r, :, part * A_GW:(part + 1) * A_GW] = res[r * n:(r + 1) * n]
    base = 3 * A_WIDTH
    for c0 in range(0, P_FB_OFF, chunk):
        pb_ref[:, c0:c0 + chunk] = _dot(h_scr[...], w_ref[:, base + c0:base + c0 + chunk]).astype(pb_ref.dtype)
    pb_ref[:, P_FB_OFF:] = _dot(h_scr[...], wfb_ref[...]).astype(pb_ref.dtype)
    for c0 in range(0, pg_ref.shape[1], chunk):
        pg_ref[:, c0:c0 + chunk] = _dot(h_scr[...], wgt_ref[:, c0:c0 + chunk]).astype(pg_ref.dtype)


def _in_proj(x, g, ws, bsz, seq, tm, dils, out_dtype):
    m, dm = x.shape
    tps = seq // tm
    w, wfb, wgt = ws
    pg_w = wgt.shape[1]
    out_specs, out_shape = [], []
    for d in dils:
        if d == 1:
            out_specs.append(pl.BlockSpec((tm, P_G), lambda i: (i, 0)))
            out_shape.append(jax.ShapeDtypeStruct((m, P_G), out_dtype))
        else:
            out_specs.append(pl.BlockSpec((1, d, tm // d, P_G), lambda i: (i // tps, 0, i % tps, 0)))
            out_shape.append(jax.ShapeDtypeStruct((bsz, d, seq // d, P_G), out_dtype))
    for n in (P_B, pg_w):
        out_specs.append(pl.BlockSpec((tm, n), lambda i: (i, 0)))
        out_shape.append(jax.ShapeDtypeStruct((m, n), out_dtype))
    perms = [d for d in dils if d > 1]
    scratch = [pltpu.VMEM((tm, dm), BF16), pltpu.VMEM((dm // LANES, tm, LANES) if perms else (1, SUBLANES, LANES), F32)]
    scratch += [pltpu.VMEM((tm, dm), BF16) for _ in perms]
    return pl.pallas_call(
        functools.partial(_in_proj_kernel, dils=tuple(dils), chunk=512),
        grid=(m // tm,),
        in_specs=[pl.BlockSpec((tm, dm), lambda i: (i, 0)), _resident(g.shape), _resident(w.shape),
                  _resident(wfb.shape), _resident(wgt.shape)],
        out_specs=out_specs,
        out_shape=out_shape,
        scratch_shapes=scratch,
        compiler_params=_cparams(("parallel",)),
        name="in_proj",
    )(x, g, w, wfb, wgt)


def _norm_matmul_kernel(x_ref, g_ref, w_ref, o_ref):
    o_ref[...] = _dot(_rms(x_ref[...], g_ref[...]).astype(BF16), w_ref[...]).astype(o_ref.dtype)


def _norm_matmul(x, g, w, tm):
    m, d = x.shape
    n = w.shape[1]
    return pl.pallas_call(
        _norm_matmul_kernel,
        grid=(m // tm,),
        in_specs=[pl.BlockSpec((tm, d), lambda i: (i, 0)), _resident(g.shape), _resident(w.shape)],
        out_specs=pl.BlockSpec((tm, n), lambda i: (i, 0)),
        out_shape=jax.ShapeDtypeStruct((m, n), F32),
        compiler_params=_cparams(("parallel",)),
        name="norm_matmul",
    )(x, g, w)


def _t5_bucket(n):
    max_exact = REL_BUCKETS // 2
    nf = jnp.maximum(n, 1).astype(F32)
    large = max_exact + (jnp.log(nf / max_exact) / math.log(REL_MAX_DIST / max_exact)
                         * (REL_BUCKETS - max_exact)).astype(jnp.int32)
    return jnp.where(n < max_exact, n, jnp.minimum(large, REL_BUCKETS - 1))


def _bias_by_steps(rel_bias, gi, dil):
    steps = jnp.arange(A_NK + 1, dtype=jnp.int32)
    return rel_bias[:, gi * A_HPG:(gi + 1) * A_HPG][_t5_bucket(steps * dil)].T.astype(F32)


def _dil_prompt_kernel(q_ref, kc_ref, kp_ref, vc_ref, vp_ref, u_ref, o_ref, l_ref, bias_scr, of_scr, lf_scr,
                       *, gi, dil, nsub):
    first = pl.program_id(1) == 0
    for h in range(A_HPG):
        bias_scr[h] = pltpu.roll(jnp.broadcast_to(u_ref[h:h + 1, :], (A_NK, 2 * A_NK)), 0, 1, stride=1, stride_axis=0)
    scale = A_DH ** -0.5
    col = lax.broadcasted_iota(jnp.int32, (A_NK, 2 * A_NK), 1)
    lane = lax.broadcasted_iota(jnp.int32, (A_NK, LANES), 1)
    for r in range(dil):
        for s in range(nsub):
            rows = slice(s * A_NK, (s + 1) * A_NK)
            dst = _strided(s * A_NK * dil + r, A_NK, dil)
            lt = jnp.zeros((A_NK, LANES), F32)
            for h in range(A_HPG):
                cols = slice(h * A_DH, (h + 1) * A_DH)
                if s == 0:
                    kw = jnp.concatenate([kp_ref[0, r, :, cols], kc_ref[0, r, rows, cols]], axis=0)
                    vw = jnp.concatenate([vp_ref[0, r, :, cols], vc_ref[0, r, rows, cols]], axis=0)
                else:
                    kw = kc_ref[0, r, (s - 1) * A_NK:(s + 1) * A_NK, cols]
                    vw = vc_ref[0, r, (s - 1) * A_NK:(s + 1) * A_NK, cols]
                sc = _dot_nt(q_ref[0, r, rows, cols], kw) * scale + bias_scr[h]
                if s == 0:
                    sc = jnp.where(jnp.logical_and(first, col < A_NK), NEG_INF, sc)
                mx = jnp.max(sc, axis=-1, keepdims=True)
                p = jnp.exp(sc - mx)
                den = jnp.sum(p, axis=-1, keepdims=True)
                o = _dot(p.astype(BF16), vw) / den
                if dil == 1:
                    o_ref[0, rows, cols] = o.astype(o_ref.dtype)
                else:
                    of_scr[h, dst, :] = o
                lt = jnp.where(lane == gi * A_HPG + h, mx + jnp.log(den), lt)
            if dil == 1:
                l_ref[0, rows, :] = lt
            else:
                lf_scr[dst, :] = lt
    if dil > 1:
        for h in range(A_HPG):
            o_ref[0, :, h * A_DH:(h + 1) * A_DH] = of_scr[h].astype(o_ref.dtype)
        l_ref[0] = lf_scr[...]


def _dil_prompt(a, u, gi, dil, bsz, seq, tt):
    m = seq // dil
    tt = min(tt, seq)
    mt = tt // dil
    nsub = mt // A_NK
    assert mt % A_NK == 0 and seq % tt == 0

    def cur(cb):
        return pl.BlockSpec((1, dil, mt, A_GW), lambda b, i: (b, 0, i, cb))

    def prev(cb):
        return pl.BlockSpec((1, dil, A_NK, A_GW), lambda b, i: (b, 0, jnp.maximum(i * nsub - 1, 0), cb))

    st = tt if dil > 1 else SUBLANES
    o, lse = pl.pallas_call(
        functools.partial(_dil_prompt_kernel, gi=gi, dil=dil, nsub=nsub),
        grid=(bsz, seq // tt),
        in_specs=[cur(0), cur(1), prev(1), cur(2), prev(2), _resident(u.shape)],
        out_specs=[pl.BlockSpec((1, tt, A_GW), lambda b, i: (b, i, 0)),
                   pl.BlockSpec((1, tt, LANES), lambda b, i: (b, i, 0))],
        out_shape=[jax.ShapeDtypeStruct((bsz, seq, A_GW), a.dtype), jax.ShapeDtypeStruct((bsz, seq, LANES), F32)],
        scratch_shapes=[pltpu.VMEM((A_HPG, A_NK, 2 * A_NK), F32), pltpu.VMEM((A_HPG, st, A_DH), F32),
                        pltpu.VMEM((st, LANES), F32)],
        compiler_params=_cparams(("parallel", "arbitrary")),
        name=f"dil_prompt_g{gi}",
    )(a, a, a, a, a, u)
    return o.reshape(bsz * seq, A_GW), lse.reshape(bsz * seq, LANES)


def _prompt_bias_row(rel_bias, gi, dil):
    tab = _bias_by_steps(rel_bias, gi, dil)
    return jnp.concatenate([tab[:, ::-1], jnp.full((A_HPG, A_NK - 1), NEG_INF, F32)], axis=1)


def _dil_sample_kernel(a1_ref, a2_ref, a3_ref, c1_ref, c2_ref, c3_ref, bc1_ref, bc2_ref, bc3_ref, bn_ref,
                       o1_ref, o2_ref, o3_ref, l1_ref, l2_ref, l3_ref):
    scale = A_DH ** -0.5
    nq = a1_ref.shape[1]
    groups = ((a1_ref, c1_ref, bc1_ref, o1_ref, l1_ref), (a2_ref, c2_ref, bc2_ref, o2_ref, l2_ref),
              (a3_ref, c3_ref, bc3_ref, o3_ref, l3_ref))
    for gi, (a_ref, c_ref, bc_ref, o_ref, l_ref) in enumerate(groups):
        rows = c_ref[0]
        if rows.ndim == 3:
            rows = rows.reshape(rows.shape[0] * rows.shape[1], rows.shape[2])
        rows = rows.astype(BF16)
        qs = [a_ref[0, :, h * A_DH:(h + 1) * A_DH] for h in range(A_HPG)]
        q_all = jnp.concatenate(qs, axis=0).astype(BF16)
        sc = _dot_nt(q_all, rows) * scale + bc_ref[...]
        scn = [_dot_nt(qs[h].astype(BF16), a_ref[0, :, A_GW + h * A_DH:A_GW + (h + 1) * A_DH].astype(BF16)) * scale
               + bn_ref[gi, h] for h in range(A_HPG)]
        scn = jnp.concatenate(scn, axis=0)
        mx = jnp.maximum(jnp.max(sc, axis=-1, keepdims=True), jnp.max(scn, axis=-1, keepdims=True))
        p = jnp.exp(sc - mx)
        pn = jnp.exp(scn - mx)
        den = jnp.sum(p, axis=-1, keepdims=True) + jnp.sum(pn, axis=-1, keepdims=True)
        acc = _dot(pltpu.roll(p, A_HPG, 1).astype(BF16), rows)
        lse = mx + jnp.log(den)
        lane = lax.broadcasted_iota(jnp.int32, (nq, LANES), 1)
        lt = jnp.zeros((nq, LANES), F32)
        for h in range(A_HPG):
            hr = slice(h * nq, (h + 1) * nq)
            vnew = a_ref[0, :, 2 * A_GW + h * A_DH:2 * A_GW + (h + 1) * A_DH].astype(BF16)
            o = (acc[hr] + _dot(pn[hr].astype(BF16), vnew)) / den[hr]
            o_ref[0, :, h * A_DH:(h + 1) * A_DH] = o.astype(o_ref.dtype)
            lt = jnp.where(lane == gi * A_HPG + h, lse[hr], lt)
        l_ref[0] = lt


def _toeplitz(v, n, width):
    lead, length = v.shape[:-1], v.shape[-1]
    t = jnp.tile(v, (1,) * len(lead) + (n,))[..., :n * (length - 1)]
    return t.reshape(lead + (n, length - 1))[..., :width]


def _sample_biases(rel_bias, nq, nres):
    cache, new = [], []
    for gi, (window, dil) in enumerate(A_GROUPS):
        tab = _bias_by_steps(rel_bias, gi, dil)
        length = window + nq
        rep = jnp.repeat(tab, dil, axis=1)
        rep = jnp.where((jnp.arange(rep.shape[1]) % dil == 0)[None], rep, NEG_INF)
        by = jnp.concatenate([rep, jnp.full((A_HPG, max(length - rep.shape[1], 0)), NEG_INF, F32)], axis=1)[:, :length]
        bc = _toeplitz(jnp.roll(by[:, ::-1], -(nq - 1), axis=1), nq, window)
        bn = _toeplitz(jnp.concatenate([by[:, :1], jnp.full((A_HPG, nq), NEG_INF, F32), by[:, nq - 1:0:-1]], axis=1),
                       nq, nq)
        own = (jnp.arange(2 * A_HPG)[None, :] == jnp.arange(A_HPG)[:, None])
        full = jnp.where(own[:, None, None, :], bc[:, :, :, None], NEG_INF)
        if gi == len(A_GROUPS) - 1:
            full = full.reshape(A_HPG, nq, window // dil, dil, 2 * A_HPG)[:, :, :, :nres]
        cache.append(full.reshape(A_HPG * nq, -1))
        new.append(bn)
    return cache, jnp.stack(new)


def _dil_sample(a_s, caches, rel_bias):
    bsz, nq, _ = a_s[0].shape
    w3, dil3 = caches[2].shape[1], A_GROUPS[2][1]
    nres = min(dil3, nq)
    assert w3 % dil3 == 0 and nq <= dil3, "new tokens must fall in distinct residue classes of the widest dilation"
    bcs, bn = _sample_biases(rel_bias, nq, nres)
    rpp = 2 * A_HPG
    c1 = caches[0].reshape(bsz, caches[0].shape[1] * rpp, A_DH)
    c2 = caches[1].reshape(bsz, caches[1].shape[1] * rpp, A_DH)
    c3 = caches[2].reshape(bsz, w3 // dil3, dil3 * rpp, A_DH)
    row = lambda b: (b, 0, 0)
    outs = pl.pallas_call(
        _dil_sample_kernel,
        grid=(bsz,),
        in_specs=[pl.BlockSpec((1, nq, P_G), row)] * 3
        + [pl.BlockSpec((1,) + c1.shape[1:], row), pl.BlockSpec((1,) + c2.shape[1:], row),
           pl.BlockSpec((1, w3 // dil3, nres * rpp, A_DH), lambda b: (b, 0, 0, 0)),
           _resident(bcs[0].shape), _resident(bcs[1].shape), _resident(bcs[2].shape), _resident(bn.shape)],
        out_specs=[pl.BlockSpec((1, nq, A_GW), row)] * 3 + [pl.BlockSpec((1, nq, LANES), row)] * 3,
        out_shape=[jax.ShapeDtypeStruct((bsz, nq, A_GW), F32)] * 3 + [jax.ShapeDtypeStruct((bsz, nq, LANES), F32)] * 3,
        compiler_params=_cparams(("parallel",)),
        name="dil_sample",
    )(*a_s, c1, c2, c3, bcs[0], bcs[1], bcs[2], bn)
    return [o.reshape(bsz * nq, o.shape[-1]) for o in outs]


def _gla_kernel(q_ref, k_ref, v_ref, gb_ref, fb_ref, wf_ref, bf_ref, gn_ref, s0_ref, o_ref, sf_ref, st_scr,
                *, chunk, nchunks):
    t = pl.program_id(1)

    @pl.when(t == 0)
    def _():
        st_scr[...] = s0_ref[0]

    tt = chunk * nchunks
    mid = chunk // 2
    z = _dot(fb_ref[0].astype(BF16), wf_ref[...]) + bf_ref[...]
    lf_all = (jnp.minimum(z, 0.0) - jnp.log(1.0 + jnp.exp(-jnp.abs(z)))) * (1.0 / B_GATE_NORM)
    blk = tt
    ri = lax.broadcasted_iota(jnp.int32, (blk, blk), 0)
    ci = lax.broadcasted_iota(jnp.int32, (blk, blk), 1)
    causal = jnp.logical_and(ri >= ci, ri // chunk == ci // chunk)
    pos = lax.broadcasted_iota(jnp.int32, (tt, B_DK), 0) % chunk
    ones_col = jnp.ones((chunk, 1), F32)
    hs = slice(0, B_DK)
    for h in range(B_HEADS):
        hcols = slice(h * B_DK, (h + 1) * B_DK)
        vs = slice(h * B_DV, (h + 1) * B_DV)
        lf = lf_all[:, hcols]
        b = lf
        sh = 1
        while sh < chunk:
            b = b + jnp.where(pos >= sh, pltpu.roll(b, sh, 0), 0.0)
            sh *= 2
        b3 =b.reshape(nchunks, chunk, B_DK)
        bm = jnp.broadcast_to(b3[:, mid:mid + 1, :], b3.shape).reshape(tt, B_DK)
        blr = b3[:, chunk - 1, :]
        bl = jnp.broadcast_to(b3[:, chunk - 1:chunk, :], b3.shape).reshape(tt, B_DK)
        q = q_ref[0, :, hcols].astype(F32) * (B_DK ** -0.5)
        k = k_ref[0, :, hcols].astype(F32)
        qe = (q * jnp.exp(b)).astype(BF16)
        qm = (q * jnp.exp(b - bm)).astype(BF16)
        km = (k * jnp.exp(bm - b)).astype(BF16)
        kl = (k * jnp.exp(bl - b)).astype(BF16)
        dec_t = jnp.exp(blr).T if nchunks >= SUBLANES else None
        v = v_ref[0, :, vs].astype(BF16)
        chunks = [slice(c * chunk, (c + 1) * chunk) for c in range(nchunks)]
        us = [_dot_tn(kl[rows, hs], v[rows]) for rows in chunks]
        a = jnp.where(causal, _dot_nt(qm[:, hs], km[:, hs]), 0.0)
        o_intra = _dot(a.astype(BF16), v)
        gbv = gb_ref[0, :, vs].astype(F32)
        gate = gbv * jax.nn.sigmoid(gbv)
        st = st_scr[h]
        sts = []
        for c, rows in enumerate(chunks):
            sts.append(st.astype(BF16))
            if dec_t is not None:
                dcol = dec_t[hs, c:c + 1]
            else:
                dcol = jnp.exp(_dot_tn(lf[rows, hs], ones_col))
            st = st * dcol + us[c]
        for c, rows in enumerate(chunks):
            o = o_intra[rows] + _dot(qe[rows, hs], sts[c])
            o_ref[0, rows, vs] = (_rms(o, gn_ref[...]) * gate[rows]).astype(o_ref.dtype)
        st_scr[h] = st

    @pl.when(t == pl.num_programs(1) - 1)
    def _():
        sf_ref[0] = st_scr[...]


def _gla(pb, wf, bf, gn, s0, bsz, seq, chunk, tt):
    pb3 = pb.reshape(bsz, seq, P_B)
    tt = min(tt, seq)
    o, sf = pl.pallas_call(
        functools.partial(_gla_kernel, chunk=chunk, nchunks=tt // chunk),
        grid=(bsz, seq // tt),
        in_specs=[pl.BlockSpec((1, tt, B_QK), lambda b, t: (b, t, 0)),
                  pl.BlockSpec((1, tt, B_QK), lambda b, t: (b, t, 1)),
                  pl.BlockSpec((1, tt, B_V), lambda b, t: (b, t, 2 * B_QK // B_V)),
                  pl.BlockSpec((1, tt, B_V), lambda b, t: (b, t, 2 * B_QK // B_V + 1)),
                  pl.BlockSpec((1, tt, LANES), lambda b, t: (b, t, P_FB_OFF // LANES)),
                  _resident(wf.shape), _resident(bf.shape), _resident(gn.shape),
                  pl.BlockSpec((1, B_HEADS, B_DK, B_DV), lambda b, t: (b, 0, 0, 0))],
        out_specs=[pl.BlockSpec((1, tt, B_V), lambda b, t: (b, t, 0)),
                   pl.BlockSpec((1, B_HEADS, B_DK, B_DV), lambda b, t: (b, 0, 0, 0))],
        out_shape=[jax.ShapeDtypeStruct((bsz, seq, B_V), pb.dtype),
                   jax.ShapeDtypeStruct((bsz, B_HEADS, B_DK, B_DV), F32)],
        scratch_shapes=[pltpu.VMEM((B_HEADS, B_DK, B_DV), F32)],
        compiler_params=_cparams(("parallel", "arbitrary")),
        name="gla",
    )(pb3, pb3, pb3, pb3, pb3, wf, bf, gn, s0)
    return o.reshape(bsz * seq, B_V), sf


def _mix_out_kernel(o1_ref, o2_ref, o3_ref, l1_ref, l2_ref, l3_ref, ob_ref, ga_ref, gbt_ref, x_ref,
                    ex_ref, wa_ref, wb_ref, wo_ref, gp_ref, y_ref, *, nsplit):
    n = x_ref.shape[0] // nsplit
    for i in range(nsplit):
        rows = slice(i * n, (i + 1) * n)
        _mix_out_rows(*(r.at[rows] for r in (o1_ref, o2_ref, o3_ref, l1_ref, l2_ref, l3_ref, ob_ref, ga_ref, gbt_ref,
                                             x_ref)), ex_ref, wa_ref, wb_ref, wo_ref, gp_ref, y_ref.at[rows])


def _mix_out_rows(o1_ref, o2_ref, o3_ref, l1_ref, l2_ref, l3_ref, ob_ref, ga_ref, gbt_ref, x_ref,
                  ex_ref, wa_ref, wb_ref, wo_ref, gp_ref, y_ref):
    yb = _dot(ob_ref[...].astype(BF16), wb_ref[...])
    ng = len(A_GROUPS)
    ls = l1_ref[...] + l2_ref[...] + l3_ref[...]
    lane = lax.broadcasted_iota(jnp.int32, ls.shape, 1)

    def over_groups(v, op):
        red = v
        for g in range(1, ng):
            red = op(red, pltpu.roll(v, LANES - g * A_HPG, 1))
        out = red
        for g in range(1, ng):
            out = jnp.where(lane >= g * A_HPG, pltpu.roll(red, g * A_HPG, 1), out)
        return out

    e = jnp.exp(ls - over_groups(ls, jnp.maximum))
    w = jnp.where(lane < ng * A_HPG, e / over_groups(e, jnp.add), 0.0)
    w_hi = w.astype(BF16)
    w_lo = (w - w_hi.astype(F32)).astype(BF16)
    wf = _dot(w_hi, ex_ref[...]) + _dot(w_lo, ex_ref[...])
    comb = (wf[:, :A_GW] * o1_ref[...].astype(F32) + wf[:, A_GW:2 * A_GW] * o2_ref[...].astype(F32)
            + wf[:, 2 * A_GW:] * o3_ref[...].astype(F32))
    ya = _dot(comb.astype(BF16), wa_ref[...])
    mix =jax.nn.sigmoid(ga_ref[...].astype(F32)) * ya + jax.nn.sigmoid(gbt_ref[...].astype(F32)) * yb
    out = _dot(mix.astype(BF16), wo_ref[...])
    y_ref[...] = x_ref[...] + _rms(out, gp_ref[...])


def _mix_out(oas, lses, ob, gates, x, wa, wb, wo, gpost, tm):
    m, d = x.shape
    row = lambda i: (i, 0)
    ex = (jnp.arange(LANES)[:, None] == jnp.arange(A_WIDTH)[None, :] // A_DH).astype(BF16)
    return pl.pallas_call(
        functools.partial(_mix_out_kernel, nsplit=2 if tm % (2 * 2 * SUBLANES) == 0 else 1),
        grid=(m // tm,),
        in_specs=[pl.BlockSpec((tm, A_GW), row)] * 3 + [pl.BlockSpec((tm, LANES), row)] * 3
        + [pl.BlockSpec((tm, B_V), row), pl.BlockSpec((tm, d), row), pl.BlockSpec((tm, d), lambda i: (i, 1)),
           pl.BlockSpec((tm, d), row), _resident(ex.shape), _resident(wa.shape), _resident(wb.shape),
           _resident(wo.shape), _resident(gpost.shape)],
        out_specs=pl.BlockSpec((tm, d), row),
        out_shape=jax.ShapeDtypeStruct((m, d), F32),
        compiler_params=_cparams(("parallel",)),
        name="mix_out",
    )(*oas, *lses, ob, gates, gates, x, ex, wa, wb, wo, gpost)


def _cross_kernel(x_ref, kv_ref, wq_ref, wo_ref, gpre_ref, gpost_ref, y_ref, q_scr, c_scr, *, nb, nsplit):
    tm = x_ref.shape[0]
    if nb == 1:
        n = tm // nsplit
        for i in range(nsplit):
            _cross_rows(x_ref, kv_ref, wq_ref, wo_ref, gpre_ref, gpost_ref, y_ref, q_scr, c_scr,
                        slice(i * n, (i + 1) * n), [(0, slice(i * n, (i + 1) * n))])
    else:
        rpb = tm // nb
        _cross_rows(x_ref, kv_ref, wq_ref, wo_ref, gpre_ref, gpost_ref, y_ref, q_scr, c_scr,
                    slice(0, tm), [(j, slice(j * rpb, (j + 1) * rpb)) for j in range(nb)])


def _cross_rows(x_ref, kv_ref, wq_ref, wo_ref, gpre_ref, gpost_ref, y_ref, q_scr, c_scr, tile, batches):
    x = x_ref[tile]
    d = x.shape[1]
    dh = d // M_HEADS
    q_scr[tile] = _dot(_rms(x, gpre_ref[...]).astype(BF16), wq_ref[...])
    scale = dh ** -0.5
    for j, rows in batches:
        rpb = rows.stop - rows.start
        if len(kv_ref.shape) == 5:
            mem = kv_ref.shape[1]
            kall = kv_ref[j, :, 0].reshape(mem * M_HEADS, dh).astype(BF16)
            vall = kv_ref[j, :, 1].reshape(mem * M_HEADS, dh).astype(BF16)
            qall = jnp.concatenate([q_scr[rows, h * dh:(h + 1) * dh] for h in range(M_HEADS)], axis=0).astype(BF16)
            rh = lax.broadcasted_iota(jnp.int32, (M_HEADS * rpb, mem * M_HEADS), 0) // rpb
            ch = lax.broadcasted_iota(jnp.int32, (M_HEADS * rpb, mem * M_HEADS), 1) % M_HEADS
            sc = jnp.where(rh == ch, _dot_nt(qall, kall) * scale, NEG_INF)
            p = jnp.exp(sc - jnp.max(sc, axis=-1, keepdims=True))
            ctx = _dot(p.astype(BF16), vall) / jnp.sum(p, axis=-1, keepdims=True)
            for h in range(M_HEADS):
                c_scr[rows, h * dh:(h + 1) * dh] = ctx[h * rpb:(h + 1) * rpb]
            continue
        for h in range(M_HEADS):
            cols = slice(h * dh, (h + 1) * dh)
            k = kv_ref[j, :, cols].astype(BF16)
            v = kv_ref[j, :, d + cols.start:d + cols.stop].astype(BF16)
            sc = _dot_nt(q_scr[rows, cols].astype(BF16), k) * scale
            p = jnp.exp(sc - jnp.max(sc, axis=-1, keepdims=True))
            den = jnp.sum(p, axis=-1, keepdims=True)
            c_scr[rows, cols] = _dot(p.astype(BF16), v) / den
    out = _dot(c_scr[tile].astype(BF16), wo_ref[...])
    y_ref[tile] = x + _rms(out, gpost_ref[...])


def _cross(x, kv, wq, wo, gpre, gpost, tm, nb, rows_per_batch):
    m, d = x.shape
    tiles_per_kv = max(rows_per_batch // tm, 1)
    kv_tail = (0,) * (kv.ndim - 1)
    return pl.pallas_call(
        functools.partial(_cross_kernel, nb=nb, nsplit=1),
        grid=(m // tm,),
        in_specs=[pl.BlockSpec((tm, d), lambda i: (i, 0)),
                  pl.BlockSpec((nb,) + kv.shape[1:], lambda i: (i // tiles_per_kv,) + kv_tail),
                  _resident(wq.shape), _resident(wo.shape), _resident(gpre.shape), _resident(gpost.shape)],
        out_specs=pl.BlockSpec((tm, d), lambda i: (i, 0)),
        out_shape=jax.ShapeDtypeStruct((m, d), F32),
        scratch_shapes=[pltpu.VMEM((tm, d), F32), pltpu.VMEM((tm, d), F32)],
        compiler_params=_cparams(("parallel",)),
        name="cross_attn",
    )(x, kv, wq, wo, gpre, gpost)


def _ffn_kernel(x_ref, wg_ref, wu_ref, wd_ref, gpre_ref, gpost_ref, y_ref, h_scr, *, chunk, nsplit):
    n = x_ref.shape[0] // nsplit
    for i in range(nsplit):
        rows = slice(i * n, (i + 1) * n)
        x = x_ref[rows]
        h_scr[rows] = _rms(x, gpre_ref[...]).astype(BF16)
        acc = jnp.zeros(x.shape, F32)
        for c0 in range(0, wg_ref.shape[1], chunk):
            g = _dot(h_scr[rows], wg_ref[:, c0:c0 + chunk])
            u = _dot(h_scr[rows], wu_ref[:, c0:c0 + chunk])
            acc = acc + _dot((g * jax.nn.sigmoid(g) * u).astype(BF16), wd_ref[c0:c0 + chunk, :])
        y_ref[rows] = x + _rms(acc, gpost_ref[...])


def _ffn(x, wg, wu, wd, gpre, gpost, tm):
    m, d = x.shape
    return pl.pallas_call(
        functools.partial(_ffn_kernel, chunk=256, nsplit=1),
        grid=(m // tm,),
        in_specs=[pl.BlockSpec((tm, d), lambda i: (i, 0)), _resident(wg.shape), _resident(wu.shape),
                  _resident(wd.shape), _resident(gpre.shape), _resident(gpost.shape)],
        out_specs=pl.BlockSpec((tm, d), lambda i: (i, 0)),
        out_shape=jax.ShapeDtypeStruct((m, d), F32),
        scratch_shapes=[pltpu.VMEM((tm, d), BF16)],
        compiler_params=_cparams(("parallel",)),
        name="ffn",
    )(x, wg, wu, wd, gpre, gpost)


def _row(v):
    return v.reshape(1, -1).astype(F32)


def _kv_tail_kernel(k_ref, v_ref, o_ref, *, dil):
    n = k_ref.shape[2]
    rpp = 2 * A_HPG
    for part, ref in enumerate((k_ref, v_ref)):
        for h in range(A_HPG):
            for r in range(dil):
                piece = ref[0, r, :, h * A_DH:(h + 1) * A_DH].astype(F32)
                o_ref[0, _strided(r * rpp + part * A_HPG + h, n, rpp * dil), :] = piece


def _kv_tail(a4, keep, tt):
    bsz, dil, m, _ = a4.shape
    tt = min(tt, keep)
    n = tt // dil
    first = (m - keep // dil) // n
    rpp = 2 * A_HPG

    def src(cb):
        return pl.BlockSpec((1, dil, n, A_GW), lambda b, i: (b, 0, first + i, cb))

    out = pl.pallas_call(
        functools.partial(_kv_tail_kernel, dil=dil),
        grid=(bsz, keep // tt),
        in_specs=[src(1), src(2)],
        out_specs=pl.BlockSpec((1, tt * rpp, A_DH), lambda b, i: (b, i, 0)),
        out_shape=jax.ShapeDtypeStruct((bsz, keep * rpp, A_DH), F32),
        compiler_params=_cparams(("parallel", "parallel")),
        name=f"kv_tail_d{dil}",
    )(a4, a4)
    return out.reshape(bsz, keep, 2, A_HPG, A_DH)


def _kv_rows(a, keep):
    bsz, seq, _ = a.shape
    return a[:, seq - keep:, A_GW:].astype(F32).reshape(bsz, keep, 2, A_HPG, A_DH)


def _trunk(x, mem_kv, s0, chunk, caches, rel_bias, lw, *, tm, act_dtype, cross_tm, cross_nb):
    (g_mix_pre, g_mix_post, w_in, w_f2, b_f2, gla_norm, w_pa, w_pb, w_out,
     g_mem_pre, g_mem_post, w_mq, w_mo, g_ffn_pre, g_ffn_post, w_gate, w_up, w_down) = lw
    bsz, seq, d = x.shape
    x2 = x.reshape(bsz * seq, d)
    if caches is None:
        dils = tuple(dil for _, dil in A_GROUPS)
        *a_g, pb, pg = _in_proj(x2, g_mix_pre, w_in, bsz, seq, tm, dils, act_dtype)
        oas, lses, wins = [], [], []
        for gi, (window, dil) in enumerate(A_GROUPS):
            a4 = a_g[gi].reshape(bsz, dil, seq // dil, P_G)
            o, l = _dil_prompt(a4, _prompt_bias_row(rel_bias, gi, dil), gi, dil, bsz, seq, A_NK * A_GROUPS[-1][1])
            oas.append(o)
            lses.append(l)
            keep = min(window, seq)
            wins.append(_kv_tail(a4, keep, 512))
    else:
        *a_g, pb, pg = _in_proj(x2, g_mix_pre, w_in, bsz, seq, tm, (1,) * len(A_GROUPS), act_dtype)
        a_s = [a.reshape(bsz, seq, P_G) for a in a_g]
        res = _dil_sample(a_s, caches, rel_bias)
        oas, lses = res[:3], res[3:]
        wins = [_kv_rows(a, seq) for a in a_s]
    ob, s_new = _gla(pb, w_f2, b_f2, gla_norm, s0, bsz, seq, chunk, 512)
    x2 = _mix_out(oas, lses, ob, pg, x2, w_pa, w_pb, w_out, g_mix_post, min(2 * tm, bsz * seq))
    x2 = _cross(x2, mem_kv, w_mq, w_mo, g_mem_pre, g_mem_post, cross_tm, cross_nb, seq)
    x2 = _ffn(x2, w_gate, w_up, w_down, g_ffn_pre, g_ffn_post, min(2 * tm, bsz * seq))
    return x2.reshape(bsz, seq, d), wins, s_new


def kernel(x_prompt, x_sample, cache_win1_kv, cache_win2_kv, cache_win3_kv, state_gla, cache_mem_kv, mem_prompt, rel_bias, norm_mix_pre, norm_mix_post, w_in, w_f2, b_f2, gla_norm, w_proj_a, w_proj_b, w_out, norm_memtok, w_mk, w_mv, norm_mem_pre, norm_mem_post, w_mq, w_mo, norm_ffn_pre, norm_ffn_post, w_ffn_gate, w_ffn_up, w_ffn_down):
    depth = w_in.shape[0]
    bsz, seq, d = x_prompt.shape
    dbs, dseq, _ = x_sample.shape
    mem_len = mem_prompt.shape[1]
    yp, ys = x_prompt, x_sample
    outs = [[] for _ in range(9)]
    for l in range(depth):
        wl = w_in[l]
        f0 = 3 * A_WIDTH + P_FB_OFF
        w_in_l = (wl[:, :f0].astype(BF16), jnp.pad(wl[:, f0:f0 + B_RANK].astype(BF16), ((0, 0), (0, LANES - B_RANK))),
                  wl[:, f0 + B_RANK:].astype(BF16))
        w_f2_l = jnp.pad(w_f2[l], ((0, LANES - B_RANK), (0, 0))).astype(BF16)
        lw = (_row(norm_mix_pre[l]), _row(norm_mix_post[l]), w_in_l, w_f2_l, _row(b_f2[l]), _row(gla_norm[l]),
              w_proj_a[l].astype(BF16), w_proj_b[l].astype(BF16), w_out[l].astype(BF16),
              _row(norm_mem_pre[l]), _row(norm_mem_post[l]), w_mq[l].astype(BF16), w_mo[l].astype(BF16),
              _row(norm_ffn_pre[l]), _row(norm_ffn_post[l]),
              w_ffn_gate[l].astype(BF16), w_ffn_up[l].astype(BF16), w_ffn_down[l].astype(BF16))
        w_mkv = jnp.concatenate([w_mk[l], w_mv[l]], axis=1).astype(BF16)
        mem_kv_p = _norm_matmul(mem_prompt.reshape(bsz * mem_len, d), _row(norm_memtok[l]), w_mkv,
                                min(256, bsz * mem_len)).reshape(bsz, mem_len, 2 * d)
        s0 = jnp.zeros((bsz, B_HEADS, B_DK, B_DV), F32)
        tm_p = min(512, seq)
        yp, win_p, gla_p = _trunk(yp, mem_kv_p, s0, min(B_CHUNK, seq), None, rel_bias, lw,
                                  tm=tm_p, act_dtype=BF16, cross_tm=tm_p, cross_nb=1)
        caches = (cache_win1_kv[l], cache_win2_kv[l], cache_win3_kv[l])
        cross_nb = min(4, dbs)
        ys, win_s, gla_s = _trunk(ys, cache_mem_kv[l], state_gla[l], dseq, caches,
                                  rel_bias, lw, tm=dbs * dseq, act_dtype=F32,
                                  cross_tm=cross_nb * dseq, cross_nb=cross_nb)
        for lst, v in zip(outs, (*win_p, gla_p, mem_kv_p.reshape(bsz, mem_len, 2, M_HEADS, d // M_HEADS), *win_s, gla_s)):
            lst.append(v)
    return (yp, ys, *[jnp.stack(o) for o in outs])
```

```python
import functools
import math

import numpy as np
import jax
import jax.numpy as jnp
from jax import lax
from jax.experimental import pallas as pl
from jax.experimental.pallas import tpu as pltpu

F32 = jnp.float32
BF16 = jnp.bfloat16

A_GROUPS = ((128, 1), (512, 4), (2048, 16))
A_HPG = 4
A_DH = 128
A_GW = A_HPG * A_DH
A_WIDTH = A_GW * len(A_GROUPS)
A_NK = 128
REL_BUCKETS = 32
REL_MAX_DIST = 2048
B_HEADS = 4
B_DK = 128
B_DV = 256
B_QK = B_HEADS * B_DK
B_V = B_HEADS * B_DV
B_RANK = 16
B_GATE_NORM = 16.0
B_CHUNK = 64
M_HEADS = 4
EPS = 1e-6
NEG_INF = -1e30

LANES = 128
SUBLANES = 8
VMEM_LIMIT_BYTES = 56 * 1024 * 1024

P_G = 3 * A_GW
P_B = 2 * B_QK + 2 * B_V + LANES
P_FB_OFF = 2 * B_QK + 2 * B_V


def _cparams(sem):
    return pltpu.CompilerParams(dimension_semantics=sem, vmem_limit_bytes=VMEM_LIMIT_BYTES)


def _resident(shape):
    nd = len(shape)
    return pl.BlockSpec(shape, lambda *_: (0,) * nd, pipeline_mode=pl.Buffered(1))


def _rms(x, g):
    return x * lax.rsqrt(jnp.mean(x * x, axis=-1, keepdims=True) + EPS) * g


def _dot(a, b):
    return jnp.dot(a, b, preferred_element_type=F32)


def _dot_nt(a, b):
    return lax.dot_general(a, b, (((1,), (1,)), ((), ())), preferred_element_type=F32)


def _dot_tn(a, b):
    return lax.dot_general(a, b, (((0,), (0,)), ((), ())), preferred_element_type=F32)


def _strided(start, size, stride):
    return pl.ds(start, size) if stride == 1 else pl.ds(start, size, stride=stride)


def _in_proj_kernel(x_ref, g_ref, w_ref, wfb_ref, wgt_ref, *refs, dils, chunk, gla):
    ng = len(dils)
    nperm = sum(d > 1 for d in dils)
    if gla:
        (wf_ref, bf_ref, gn_ref, s0_ref), refs = refs[:4], refs[4:]
        a_refs, (pg_ref, ob_ref, sf_ref), scr = refs[:ng], refs[ng:ng + 3], refs[ng + 3:]
        pb_ref, st_scr = scr[2 + nperm], scr[3 + nperm]
        gla_chunk, tps = gla
        t = pl.program_id(0) % tps

        @pl.when(t == 0)
        def _():
            st_scr[...] = s0_ref[0]
    else:
        a_refs, (pb_ref, pg_ref), scr = refs[:ng], refs[ng:ng + 2], refs[ng + 2:]
    h_scr, hs_scr = scr[0], scr[1]
    perm_scr = dict(zip([d for d in dils if d > 1], scr[2:2 + nperm]))
    tm, dm = x_ref.shape
    h = _rms(x_ref[...], g_ref[...])
    h_scr[...] = h.astype(BF16)
    if perm_scr:
        for s in range(dm // LANES):
            hs_scr[s] = h[:, s * LANES:(s + 1) * LANES]
        for d, hp in perm_scr.items():
            n = tm // d
            for s in range(dm // LANES):
                for r in range(d):
                    hp[r * n:(r + 1) * n, s * LANES:(s + 1) * LANES] = hs_scr[s, pl.ds(r, n, stride=d), :].astype(BF16)
    base = 3 * A_WIDTH
    for c0 in range(0, P_FB_OFF, chunk):
        pb_ref[:, c0:c0 + chunk] = _dot(h_scr[...], w_ref[:, base + c0:base + c0 + chunk]).astype(pb_ref.dtype)
    pb_ref[:, P_FB_OFF:] = _dot(h_scr[...], wfb_ref[...]).astype(pb_ref.dtype)
    def group_task(gi, a_ref, d, part):
        def run():
            lhs = h_scr if d == 1 else perm_scr[d]
            n = tm // d
            src = part * A_WIDTH + gi * A_GW
            res = _dot(lhs[...], w_ref[:, src:src + A_GW]).astype(a_ref.dtype)
            if d == 1:
                a_ref[:, part * A_GW:(part + 1) * A_GW] = res
            else:
                for r in range(d):
                    a_ref[0, r, :, part * A_GW:(part + 1) * A_GW] = res[r * n:(r + 1) * n]
        return run

    def gate_task(c0):
        def run():
            pg_ref[:, c0:c0 + chunk] = _dot(h_scr[...], wgt_ref[:, c0:c0 + chunk]).astype(pg_ref.dtype)
        return run

    tasks = [group_task(gi, a_ref, d, part) for gi, (a_ref, d) in enumerate(zip(a_refs, dils)) for part in range(3)]
    tasks += [gate_task(c0) for c0 in range(0, pg_ref.shape[1], chunk)]
    if gla:
        share = -(-len(tasks) // B_HEADS)

        def before_head(h):
            for task in tasks[h * share:(h + 1) * share]:
                task()

        _gla_tile(pb_ref.at[:, 0:B_QK], pb_ref.at[:, B_QK:2 * B_QK], pb_ref.at[:, 2 * B_QK:2 * B_QK + B_V],
                  pb_ref.at[:, 2 * B_QK + B_V:P_FB_OFF], pb_ref.at[:, P_FB_OFF:], wf_ref, bf_ref, gn_ref, ob_ref, st_scr,
                  gla_chunk, tm // gla_chunk, before_head)
    else:
        for task in tasks:
            task()
    if gla:
        @pl.when(t == tps - 1)
        def _():
            sf_ref[0] = st_scr[...]


def _in_proj(x, g, ws, bsz, seq, tm, dils, out_dtype, gla=None):
    m, dm = x.shape
    tps = seq // tm
    w, wfb, wgt = ws
    pg_w = wgt.shape[1]
    row = lambda i: (i, 0)
    out_specs, out_shape = [], []
    for d in dils:
        if d == 1:
            out_specs.append(pl.BlockSpec((tm, P_G), row))
            out_shape.append(jax.ShapeDtypeStruct((m, P_G), out_dtype))
        else:
            out_specs.append(pl.BlockSpec((1, d, tm // d, P_G), lambda i: (i // tps, 0, i % tps, 0)))
            out_shape.append(jax.ShapeDtypeStruct((bsz, d, seq // d, P_G), out_dtype))
    perms = [d for d in dils if d > 1]
    scratch = [pltpu.VMEM((tm, dm), BF16), pltpu.VMEM((dm // LANES, tm, LANES) if perms else (1, SUBLANES, LANES), F32)]
    scratch += [pltpu.VMEM((tm, dm), BF16) for _ in perms]
    in_specs = [pl.BlockSpec((tm, dm), row), _resident(g.shape), _resident(w.shape), _resident(wfb.shape),
                _resident(wgt.shape)]
    args = [x, g, w, wfb, wgt]
    if gla is None:
        widths, cfg, sem = (P_B, pg_w), None, ("parallel",)
    else:
        wf, bf, gn, s0, gla_chunk = gla
        state_spec = pl.BlockSpec((1,) + s0.shape[1:], lambda i: (i // tps, 0, 0, 0))
        in_specs += [_resident(wf.shape), _resident(bf.shape), _resident(gn.shape), state_spec]
        args += [wf, bf, gn, s0]
        scratch += [pltpu.VMEM((tm, P_B), out_dtype), pltpu.VMEM(s0.shape[1:], F32)]
        widths, cfg, sem = (pg_w, B_V), (gla_chunk, tps), ("arbitrary",)
    for n in widths:
        out_specs.append(pl.BlockSpec((tm, n), row))
        out_shape.append(jax.ShapeDtypeStruct((m, n), out_dtype))
    if gla is not None:
        out_specs.append(state_spec)
        out_shape.append(jax.ShapeDtypeStruct(s0.shape, F32))
    return pl.pallas_call(
        functools.partial(_in_proj_kernel, dils=tuple(dils), chunk=512, gla=cfg),
        grid=(m // tm,),
        in_specs=in_specs,
        out_specs=out_specs,
        out_shape=out_shape,
        scratch_shapes=scratch,
        compiler_params=_cparams(sem),
        name="in_proj" if gla is None else "in_proj_gla",
    )(*args)


def _norm_matmul_kernel(x_ref, g_ref, w_ref, o_ref):
    o_ref[...] = _dot(_rms(x_ref[...], g_ref[...]).astype(BF16), w_ref[...]).astype(o_ref.dtype)


def _norm_matmul(x, g, w, tm):
    m, d = x.shape
    n = w.shape[1]
    return pl.pallas_call(
        _norm_matmul_kernel,
        grid=(m // tm,),
        in_specs=[pl.BlockSpec((tm, d), lambda i: (i, 0)), _resident(g.shape), _resident(w.shape)],
        out_specs=pl.BlockSpec((tm, n), lambda i: (i, 0)),
        out_shape=jax.ShapeDtypeStruct((m, n), F32),
        compiler_params=_cparams(("parallel",)),
        name="norm_matmul",
    )(x, g, w)


def _t5_bucket(n):
    max_exact = REL_BUCKETS // 2
    nf = jnp.maximum(n, 1).astype(F32)
    large = max_exact + (jnp.log(nf / max_exact) / math.log(REL_MAX_DIST / max_exact)
                         * (REL_BUCKETS - max_exact)).astype(jnp.int32)
    return jnp.where(n < max_exact, n, jnp.minimum(large, REL_BUCKETS - 1))


def _bias_by_steps(rel_bias, gi, dil):
    steps = jnp.arange(A_NK + 1, dtype=jnp.int32)
    return rel_bias[:, gi * A_HPG:(gi + 1) * A_HPG][_t5_bucket(steps * dil)].T.astype(F32)


def _dil_prompt_kernel(q_ref, kc_ref, kp_ref, vc_ref, vp_ref, u_ref, o_ref, l_ref, bias_scr, of_scr, lf_scr,
                       *, gi, dil, nsub):
    first = pl.program_id(1) == 0
    for h in range(A_HPG):
        bias_scr[h] = pltpu.roll(jnp.broadcast_to(u_ref[h:h + 1, :], (A_NK, 2 * A_NK)), 0, 1, stride=1, stride_axis=0)
    scale = A_DH ** -0.5
    col = lax.broadcasted_iota(jnp.int32, (A_NK, 2 * A_NK), 1)
    lane = lax.broadcasted_iota(jnp.int32, (A_NK, LANES), 1)
    for r in range(dil):
        for s in range(nsub):
            rows = slice(s * A_NK, (s + 1) * A_NK)
            dst = _strided(s * A_NK * dil + r, A_NK, dil)
            lt = jnp.zeros((A_NK, LANES), F32)
            for h in range(A_HPG):
                cols = slice(h * A_DH, (h + 1) * A_DH)
                if s == 0:
                    kw = jnp.concatenate([kp_ref[0, r, :, cols], kc_ref[0, r, rows, cols]], axis=0)
                    vw = jnp.concatenate([vp_ref[0, r, :, cols], vc_ref[0, r, rows, cols]], axis=0)
                else:
                    kw = kc_ref[0, r, (s - 1) * A_NK:(s + 1) * A_NK, cols]
                    vw = vc_ref[0, r, (s - 1) * A_NK:(s + 1) * A_NK, cols]
                sc = _dot_nt(q_ref[0, r, rows, cols], kw) * scale + bias_scr[h]
                if s == 0:
                    sc = jnp.where(jnp.logical_and(first, col < A_NK), NEG_INF, sc)
                mx = jnp.max(sc, axis=-1, keepdims=True)
                p = jnp.exp(sc - mx)
                den = jnp.sum(p, axis=-1, keepdims=True)
                o = _dot(p.astype(BF16), vw) / den
                if dil == 1:
                    o_ref[0, rows, cols] = o.astype(o_ref.dtype)
                else:
                    of_scr[h, dst, :] = o
                lt = jnp.where(lane == gi * A_HPG + h, mx + jnp.log(den), lt)
            if dil == 1:
                l_ref[0, rows, :] = lt
            else:
                lf_scr[dst, :] = lt
    if dil > 1:
        for h in range(A_HPG):
            o_ref[0, :, h * A_DH:(h + 1) * A_DH] = of_scr[h].astype(o_ref.dtype)
        l_ref[0] = lf_scr[...]


def _dil_prompt(a, u, gi, dil, bsz, seq, tt):
    m = seq // dil
    tt = min(tt, seq)
    mt = tt // dil
    nsub = mt // A_NK
    assert mt % A_NK == 0 and seq % tt == 0

    def cur(cb):
        return pl.BlockSpec((1, dil, mt, A_GW), lambda b, i: (b, 0, i, cb))

    def prev(cb):
        return pl.BlockSpec((1, dil, A_NK, A_GW), lambda b, i: (b, 0, jnp.maximum(i * nsub - 1, 0), cb))

    st = tt if dil > 1 else SUBLANES
    o, lse = pl.pallas_call(
        functools.partial(_dil_prompt_kernel, gi=gi, dil=dil, nsub=nsub),
        grid=(bsz, seq // tt),
        in_specs=[cur(0), cur(1), prev(1), cur(2), prev(2), _resident(u.shape)],
        out_specs=[pl.BlockSpec((1, tt, A_GW), lambda b, i: (b, i, 0)),
                   pl.BlockSpec((1, tt, LANES), lambda b, i: (b, i, 0))],
        out_shape=[jax.ShapeDtypeStruct((bsz, seq, A_GW), a.dtype), jax.ShapeDtypeStruct((bsz, seq, LANES), F32)],
        scratch_shapes=[pltpu.VMEM((A_HPG, A_NK, 2 * A_NK), F32), pltpu.VMEM((A_HPG, st, A_DH), F32),
                        pltpu.VMEM((st, LANES), F32)],
        compiler_params=_cparams(("parallel", "arbitrary")),
        name=f"dil_prompt_g{gi}",
    )(a, a, a, a, a, u)
    return o.reshape(bsz * seq, A_GW), lse.reshape(bsz * seq, LANES)


def _prompt_bias_row(rel_bias, gi, dil):
    tab = _bias_by_steps(rel_bias, gi, dil)
    return jnp.concatenate([tab[:, ::-1], jnp.full((A_HPG, A_NK - 1), NEG_INF, F32)], axis=1)


def _dil_sample_kernel(a1_ref, a2_ref, a3_ref, c1_ref, c2_ref, c3_ref, bc1_ref, bc2_ref, bc3_ref, bn_ref,
                       o1_ref, o2_ref, o3_ref, l1_ref, l2_ref, l3_ref):
    scale = A_DH ** -0.5
    nq = a1_ref.shape[1]
    groups = ((a1_ref, c1_ref, bc1_ref, o1_ref, l1_ref), (a2_ref, c2_ref, bc2_ref, o2_ref, l2_ref),
              (a3_ref, c3_ref, bc3_ref, o3_ref, l3_ref))
    for gi, (a_ref, c_ref, bc_ref, o_ref, l_ref) in enumerate(groups):
        rows = c_ref[0]
        if rows.ndim == 3:
            rows = rows.reshape(rows.shape[0] * rows.shape[1], rows.shape[2])
        rows = rows.astype(BF16)
        qs = [a_ref[0, :, h * A_DH:(h + 1) * A_DH] for h in range(A_HPG)]
        q_all = jnp.concatenate(qs, axis=0).astype(BF16)
        sc = _dot_nt(q_all, rows) * scale + bc_ref[...]
        scn = [_dot_nt(qs[h].astype(BF16), a_ref[0, :, A_GW + h * A_DH:A_GW + (h + 1) * A_DH].astype(BF16)) * scale
               + bn_ref[gi, h] for h in range(A_HPG)]
        scn = jnp.concatenate(scn, axis=0)
        mx = jnp.maximum(jnp.max(sc, axis=-1, keepdims=True), jnp.max(scn, axis=-1, keepdims=True))
        p = jnp.exp(sc - mx)
        pn = jnp.exp(scn - mx)
        den = jnp.sum(p, axis=-1, keepdims=True) + jnp.sum(pn, axis=-1, keepdims=True)
        acc = _dot(pltpu.roll(p, A_HPG, 1).astype(BF16), rows)
        lse = mx + jnp.log(den)
        lane = lax.broadcasted_iota(jnp.int32, (nq, LANES), 1)
        lt = jnp.zeros((nq, LANES), F32)
        for h in range(A_HPG):
            hr = slice(h * nq, (h + 1) * nq)
            vnew = a_ref[0, :, 2 * A_GW + h * A_DH:2 * A_GW + (h + 1) * A_DH].astype(BF16)
            o = (acc[hr] + _dot(pn[hr].astype(BF16), vnew)) / den[hr]
            o_ref[0, :, h * A_DH:(h + 1) * A_DH] = o.astype(o_ref.dtype)
            lt = jnp.where(lane == gi * A_HPG + h, lse[hr], lt)
        l_ref[0] = lt


def _toeplitz(v, n, width):
    lead, length = v.shape[:-1], v.shape[-1]
    t = jnp.tile(v, (1,) * len(lead) + (n,))[..., :n * (length - 1)]
    return t.reshape(lead + (n, length - 1))[..., :width]


def _sample_biases(rel_bias, nq, nres):
    cache, new = [], []
    for gi, (window, dil) in enumerate(A_GROUPS):
        tab = _bias_by_steps(rel_bias, gi, dil)
        length = window + nq
        rep = jnp.repeat(tab, dil, axis=1)
        rep = jnp.where((jnp.arange(rep.shape[1]) % dil == 0)[None], rep, NEG_INF)
        by = jnp.concatenate([rep, jnp.full((A_HPG, max(length - rep.shape[1], 0)), NEG_INF, F32)], axis=1)[:, :length]
        bc = _toeplitz(jnp.roll(by[:, ::-1], -(nq - 1), axis=1), nq, window)
        bn = _toeplitz(jnp.concatenate([by[:, :1], jnp.full((A_HPG, nq), NEG_INF, F32), by[:, nq - 1:0:-1]], axis=1),
                       nq, nq)
        own = (jnp.arange(2 * A_HPG)[None, :] == jnp.arange(A_HPG)[:, None])
        full = jnp.where(own[:, None, None, :], bc[:, :, :, None], NEG_INF)
        if gi == len(A_GROUPS) - 1:
            full = full.reshape(A_HPG, nq, window // dil, dil, 2 * A_HPG)[:, :, :, :nres]
        cache.append(full.reshape(A_HPG * nq, -1))
        new.append(bn)
    return cache, jnp.stack(new)


def _dil_sample(a_s, caches, rel_bias):
    bsz, nq, _ = a_s[0].shape
    w3, dil3 = caches[2].shape[1], A_GROUPS[2][1]
    nres = min(dil3, nq)
    assert w3 % dil3 == 0 and nq <= dil3, "new tokens must fall in distinct residue classes of the widest dilation"
    bcs, bn = _sample_biases(rel_bias, nq, nres)
    rpp = 2 * A_HPG
    c1 = caches[0].reshape(bsz, caches[0].shape[1] * rpp, A_DH)
    c2 = caches[1].reshape(bsz, caches[1].shape[1] * rpp, A_DH)
    c3 = caches[2].reshape(bsz, w3 // dil3, dil3 * rpp, A_DH)
    row = lambda b: (b, 0, 0)
    outs = pl.pallas_call(
        _dil_sample_kernel,
        grid=(bsz,),
        in_specs=[pl.BlockSpec((1, nq, P_G), row)] * 3
        + [pl.BlockSpec((1,) + c1.shape[1:], row), pl.BlockSpec((1,) + c2.shape[1:], row),
           pl.BlockSpec((1, w3 // dil3, nres * rpp, A_DH), lambda b: (b, 0, 0, 0)),
           _resident(bcs[0].shape), _resident(bcs[1].shape), _resident(bcs[2].shape), _resident(bn.shape)],
        out_specs=[pl.BlockSpec((1, nq, A_GW), row)] * 3 + [pl.BlockSpec((1, nq, LANES), row)] * 3,
        out_shape=[jax.ShapeDtypeStruct((bsz, nq, A_GW), F32)] * 3 + [jax.ShapeDtypeStruct((bsz, nq, LANES), F32)] * 3,
        compiler_params=_cparams(("parallel",)),
        name="dil_sample",
    )(*a_s, c1, c2, c3, bcs[0], bcs[1], bcs[2], bn)
    return [o.reshape(bsz * nq, o.shape[-1]) for o in outs]


def _gla_kernel(q_ref, k_ref, v_ref, gb_ref, fb_ref, wf_ref, bf_ref, gn_ref, s0_ref, o_ref, sf_ref, st_scr,
                *, chunk, nchunks):
    t = pl.program_id(1)

    @pl.when(t == 0)
    def _():
        st_scr[...] = s0_ref[0]

    _gla_tile(q_ref.at[0], k_ref.at[0], v_ref.at[0], gb_ref.at[0], fb_ref.at[0], wf_ref, bf_ref, gn_ref, o_ref.at[0],
              st_scr, chunk, nchunks)

    @pl.when(t == pl.num_programs(1) - 1)
    def _():
        sf_ref[0] = st_scr[...]


def _gla_tile(q_ref, k_ref, v_ref, gb_ref, fb_ref, wf_ref, bf_ref, gn_ref, o_ref, st_scr, chunk, nchunks,
              before_head=None):
    tt = chunk * nchunks
    mid = chunk // 2
    z = _dot(fb_ref[...].astype(BF16), wf_ref[...]) + bf_ref[...]
    lf_all = (jnp.minimum(z, 0.0) - jnp.log(1.0 + jnp.exp(-jnp.abs(z)))) * (1.0 / B_GATE_NORM)
    blk = tt
    ri = lax.broadcasted_iota(jnp.int32, (blk, blk), 0)
    ci = lax.broadcasted_iota(jnp.int32, (blk, blk), 1)
    causal = jnp.logical_and(ri >= ci, ri // chunk == ci // chunk)
    pos = lax.broadcasted_iota(jnp.int32, (tt, B_DK), 0) % chunk
    ones_col = jnp.ones((chunk, 1), F32)
    hs = slice(0, B_DK)
    for h in range(B_HEADS):
        if before_head is not None:
            before_head(h)
        hcols = slice(h * B_DK, (h + 1) * B_DK)
        vs = slice(h * B_DV, (h + 1) * B_DV)
        lf = lf_all[:, hcols]
        b = lf
        sh = 1
        while sh < chunk:
            b = b + jnp.where(pos >= sh, pltpu.roll(b, sh, 0), 0.0)
            sh *= 2
        b3 =b.reshape(nchunks, chunk, B_DK)
        bm = jnp.broadcast_to(b3[:, mid:mid + 1, :], b3.shape).reshape(tt, B_DK)
        blr = b3[:, chunk - 1, :]
        bl = jnp.broadcast_to(b3[:, chunk - 1:chunk, :], b3.shape).reshape(tt, B_DK)
        q = q_ref[:, hcols].astype(F32) * (B_DK ** -0.5)
        k = k_ref[:, hcols].astype(F32)
        qe = (q * jnp.exp(b)).astype(BF16)
        qm = (q * jnp.exp(b - bm)).astype(BF16)
        km = (k * jnp.exp(bm - b)).astype(BF16)
        kl = (k * jnp.exp(bl - b)).astype(BF16)
        dec_t = jnp.exp(blr).T if nchunks >= SUBLANES else None
        v = v_ref[:, vs].astype(BF16)
        chunks = [slice(c * chunk, (c + 1) * chunk) for c in range(nchunks)]
        us = [_dot_tn(kl[rows, hs], v[rows]) for rows in chunks]
        a = jnp.where(causal, _dot_nt(qm[:, hs], km[:, hs]), 0.0)
        o_intra = _dot(a.astype(BF16), v)
        gbv = gb_ref[:, vs].astype(F32)
        gate = gbv * jax.nn.sigmoid(gbv)
        st = st_scr[h]
        sts = []
        for c, rows in enumerate(chunks):
            sts.append(st.astype(BF16))
            if dec_t is not None:
                dcol = dec_t[hs, c:c + 1]
            else:
                dcol = jnp.exp(_dot_tn(lf[rows, hs], ones_col))
            st = st * dcol + us[c]
        for c, rows in enumerate(chunks):
            o = o_intra[rows] + _dot(qe[rows, hs], sts[c])
            o_ref[rows, vs] = (_rms(o, gn_ref[...]) * gate[rows]).astype(o_ref.dtype)
        st_scr[h] = st


def _gla(pb, wf, bf, gn, s0, bsz, seq, chunk, tt):
    pb3 = pb.reshape(bsz, seq, P_B)
    tt = min(tt, seq)
    o, sf = pl.pallas_call(
        functools.partial(_gla_kernel, chunk=chunk, nchunks=tt // chunk),
        grid=(bsz, seq // tt),
        in_specs=[pl.BlockSpec((1, tt, B_QK), lambda b, t: (b, t, 0)),
                  pl.BlockSpec((1, tt, B_QK), lambda b, t: (b, t, 1)),
                  pl.BlockSpec((1, tt, B_V), lambda b, t: (b, t, 2 * B_QK // B_V)),
                  pl.BlockSpec((1, tt, B_V), lambda b, t: (b, t, 2 * B_QK // B_V + 1)),
                  pl.BlockSpec((1, tt, LANES), lambda b, t: (b, t, P_FB_OFF // LANES)),
                  _resident(wf.shape), _resident(bf.shape), _resident(gn.shape),
                  pl.BlockSpec((1, B_HEADS, B_DK, B_DV), lambda b, t: (b, 0, 0, 0))],
        out_specs=[pl.BlockSpec((1, tt, B_V), lambda b, t: (b, t, 0)),
                   pl.BlockSpec((1, B_HEADS, B_DK, B_DV), lambda b, t: (b, 0, 0, 0))],
        out_shape=[jax.ShapeDtypeStruct((bsz, seq, B_V), pb.dtype),
                   jax.ShapeDtypeStruct((bsz, B_HEADS, B_DK, B_DV), F32)],
        scratch_shapes=[pltpu.VMEM((B_HEADS, B_DK, B_DV), F32)],
        compiler_params=_cparams(("parallel", "arbitrary")),
        name="gla",
    )(pb3, pb3, pb3, pb3, pb3, wf, bf, gn, s0)
    return o.reshape(bsz * seq, B_V), sf


def _mix_out_kernel(o1_ref, o2_ref, o3_ref, l1_ref, l2_ref, l3_ref, ob_ref, ga_ref, gbt_ref, x_ref,
                    ex_ref, wa_ref, wb_ref, wo_ref, gp_ref, y_ref, *, nsplit):
    n = x_ref.shape[0] // nsplit
    for i in range(nsplit):
        rows = slice(i * n, (i + 1) * n)
        _mix_out_rows(*(r.at[rows] for r in (o1_ref, o2_ref, o3_ref, l1_ref, l2_ref, l3_ref, ob_ref, ga_ref, gbt_ref,
                                             x_ref)), ex_ref, wa_ref, wb_ref, wo_ref, gp_ref, y_ref.at[rows])


def _mix_out_rows(o1_ref, o2_ref, o3_ref, l1_ref, l2_ref, l3_ref, ob_ref, ga_ref, gbt_ref, x_ref,
                  ex_ref, wa_ref, wb_ref, wo_ref, gp_ref, y_ref):
    yb = _dot(ob_ref[...].astype(BF16), wb_ref[...])
    ng = len(A_GROUPS)
    ls = l1_ref[...] + l2_ref[...] + l3_ref[...]
    lane = lax.broadcasted_iota(jnp.int32, ls.shape, 1)

    def over_groups(v, op):
        red = v
        for g in range(1, ng):
            red = op(red, pltpu.roll(v, LANES - g * A_HPG, 1))
        out = red
        for g in range(1, ng):
            out = jnp.where(lane >= g * A_HPG, pltpu.roll(red, g * A_HPG, 1), out)
        return out

    e = jnp.exp(ls - over_groups(ls, jnp.maximum))
    w = jnp.where(lane < ng * A_HPG, e / over_groups(e, jnp.add), 0.0)
    w_hi = w.astype(BF16)
    w_lo = (w - w_hi.astype(F32)).astype(BF16)
    wf = _dot(w_hi, ex_ref[...]) + _dot(w_lo, ex_ref[...])
    comb = (wf[:, :A_GW] * o1_ref[...].astype(F32) + wf[:, A_GW:2 * A_GW] * o2_ref[...].astype(F32)
            + wf[:, 2 * A_GW:] * o3_ref[...].astype(F32))
    ya = _dot(comb.astype(BF16), wa_ref[...])
    mix =jax.nn.sigmoid(ga_ref[...].astype(F32)) * ya + jax.nn.sigmoid(gbt_ref[...].astype(F32)) * yb
    out = _dot(mix.astype(BF16), wo_ref[...])
    y_ref[...] = x_ref[...] + _rms(out, gp_ref[...])


def _mix_out(oas, lses, ob, gates, x, wa, wb, wo, gpost, tm):
    m, d = x.shape
    row = lambda i: (i, 0)
    ex = (jnp.arange(LANES)[:, None] == jnp.arange(A_WIDTH)[None, :] // A_DH).astype(BF16)
    return pl.pallas_call(
        functools.partial(_mix_out_kernel, nsplit=2 if tm % (2 * 2 * SUBLANES) == 0 else 1),
        grid=(m // tm,),
        in_specs=[pl.BlockSpec((tm, A_GW), row)] * 3 + [pl.BlockSpec((tm, LANES), row)] * 3
        + [pl.BlockSpec((tm, B_V), row), pl.BlockSpec((tm, d), row), pl.BlockSpec((tm, d), lambda i: (i, 1)),
           pl.BlockSpec((tm, d), row), _resident(ex.shape), _resident(wa.shape), _resident(wb.shape),
           _resident(wo.shape), _resident(gpost.shape)],
        out_specs=pl.BlockSpec((tm, d), row),
        out_shape=jax.ShapeDtypeStruct((m, d), F32),
        compiler_params=_cparams(("parallel",)),
        name="mix_out",
    )(*oas, *lses, ob, gates, gates, x, ex, wa, wb, wo, gpost)


def _cross_kernel(x_ref, kv_ref, wq_ref, wo_ref, gpre_ref, gpost_ref, y_ref, q_scr, c_scr, *, nb, nsplit):
    tm = x_ref.shape[0]
    if nb == 1:
        n = tm // nsplit
        for i in range(nsplit):
            _cross_rows(x_ref, kv_ref, wq_ref, wo_ref, gpre_ref, gpost_ref, y_ref, q_scr, c_scr,
                        slice(i * n, (i + 1) * n), [(0, slice(i * n, (i + 1) * n))])
    else:
        rpb = tm // nb
        _cross_rows(x_ref, kv_ref, wq_ref, wo_ref, gpre_ref, gpost_ref, y_ref, q_scr, c_scr,
                    slice(0, tm), [(j, slice(j * rpb, (j + 1) * rpb)) for j in range(nb)])


def _cross_rows(x_ref, kv_ref, wq_ref, wo_ref, gpre_ref, gpost_ref, y_ref, q_scr, c_scr, tile, batches):
    x = x_ref[tile]
    d = x.shape[1]
    dh = d // M_HEADS
    q_scr[tile] = _dot(_rms(x, gpre_ref[...]).astype(BF16), wq_ref[...])
    scale = dh ** -0.5
    for j, rows in batches:
        rpb = rows.stop - rows.start
        if len(kv_ref.shape) == 5:
            mem = kv_ref.shape[1]
            kall = kv_ref[j, :, 0].reshape(mem * M_HEADS, dh).astype(BF16)
            vall = kv_ref[j, :, 1].reshape(mem * M_HEADS, dh).astype(BF16)
            qall = jnp.concatenate([q_scr[rows, h * dh:(h + 1) * dh] for h in range(M_HEADS)], axis=0).astype(BF16)
            rh = lax.broadcasted_iota(jnp.int32, (M_HEADS * rpb, mem * M_HEADS), 0) // rpb
            ch = lax.broadcasted_iota(jnp.int32, (M_HEADS * rpb, mem * M_HEADS), 1) % M_HEADS
            sc = jnp.where(rh == ch, _dot_nt(qall, kall) * scale, NEG_INF)
            p = jnp.exp(sc - jnp.max(sc, axis=-1, keepdims=True))
            ctx = _dot(p.astype(BF16), vall) / jnp.sum(p, axis=-1, keepdims=True)
            for h in range(M_HEADS):
                c_scr[rows, h * dh:(h + 1) * dh] = ctx[h * rpb:(h + 1) * rpb]
            continue
        for h in range(M_HEADS):
            cols = slice(h * dh, (h + 1) * dh)
            k = kv_ref[j, :, cols].astype(BF16)
            v = kv_ref[j, :, d + cols.start:d + cols.stop].astype(BF16)
            sc = _dot_nt(q_scr[rows, cols].astype(BF16), k) * scale
            p = jnp.exp(sc - jnp.max(sc, axis=-1, keepdims=True))
            den = jnp.sum(p, axis=-1, keepdims=True)
            c_scr[rows, cols] = _dot(p.astype(BF16), v) / den
    out = _dot(c_scr[tile].astype(BF16), wo_ref[...])
    y_ref[tile] = x + _rms(out, gpost_ref[...])


def _cross(x, kv, wq, wo, gpre, gpost, tm, nb, rows_per_batch):
    m, d = x.shape
    tiles_per_kv = max(rows_per_batch // tm, 1)
    kv_tail = (0,) * (kv.ndim - 1)
    return pl.pallas_call(
        functools.partial(_cross_kernel, nb=nb, nsplit=1),
        grid=(m // tm,),
        in_specs=[pl.BlockSpec((tm, d), lambda i: (i, 0)),
                  pl.BlockSpec((nb,) + kv.shape[1:], lambda i: (i // tiles_per_kv,) + kv_tail),
                  _resident(wq.shape), _resident(wo.shape), _resident(gpre.shape), _resident(gpost.shape)],
        out_specs=pl.BlockSpec((tm, d), lambda i: (i, 0)),
        out_shape=jax.ShapeDtypeStruct((m, d), F32),
        scratch_shapes=[pltpu.VMEM((tm, d), F32), pltpu.VMEM((tm, d), F32)],
        compiler_params=_cparams(("parallel",)),
        name="cross_attn",
    )(x, kv, wq, wo, gpre, gpost)


def _ffn_kernel(x_ref, wg_ref, wu_ref, wd_ref, gpre_ref, gpost_ref, y_ref, h_scr, *, chunk, nsplit):
    n = x_ref.shape[0] // nsplit
    for i in range(nsplit):
        rows = slice(i * n, (i + 1) * n)
        x = x_ref[rows]
        h_scr[rows] = _rms(x, gpre_ref[...]).astype(BF16)
        acc = jnp.zeros(x.shape, F32)
        for c0 in range(0, wg_ref.shape[1], chunk):
            g = _dot(h_scr[rows], wg_ref[:, c0:c0 + chunk])
            u = _dot(h_scr[rows], wu_ref[:, c0:c0 + chunk])
            acc = acc + _dot((g * jax.nn.sigmoid(g) * u).astype(BF16), wd_ref[c0:c0 + chunk, :])
        y_ref[rows] = x + _rms(acc, gpost_ref[...])


def _ffn(x, wg, wu, wd, gpre, gpost, tm):
    m, d = x.shape
    return pl.pallas_call(
        functools.partial(_ffn_kernel, chunk=256, nsplit=1),
        grid=(m // tm,),
        in_specs=[pl.BlockSpec((tm, d), lambda i: (i, 0)), _resident(wg.shape), _resident(wu.shape),
                  _resident(wd.shape), _resident(gpre.shape), _resident(gpost.shape)],
        out_specs=pl.BlockSpec((tm, d), lambda i: (i, 0)),
        out_shape=jax.ShapeDtypeStruct((m, d), F32),
        scratch_shapes=[pltpu.VMEM((tm, d), BF16)],
        compiler_params=_cparams(("parallel",)),
        name="ffn",
    )(x, wg, wu, wd, gpre, gpost)


def _row(v):
    return v.reshape(1, -1).astype(F32)


def _kv_tail_kernel(k_ref, v_ref, o_ref, *, dil):
    n = k_ref.shape[2]
    rpp = 2 * A_HPG
    for part, ref in enumerate((k_ref, v_ref)):
        for h in range(A_HPG):
            for r in range(dil):
                piece = ref[0, r, :, h * A_DH:(h + 1) * A_DH].astype(F32)
                o_ref[0, _strided(r * rpp + part * A_HPG + h, n, rpp * dil), :] = piece


def _kv_tail(a4, keep, tt):
    bsz, dil, m, _ = a4.shape
    tt = min(tt, keep)
    n = tt // dil
    first = (m - keep // dil) // n
    rpp = 2 * A_HPG

    def src(cb):
        return pl.BlockSpec((1, dil, n, A_GW), lambda b, i: (b, 0, first + i, cb))

    out = pl.pallas_call(
        functools.partial(_kv_tail_kernel, dil=dil),
        grid=(bsz, keep // tt),
        in_specs=[src(1), src(2)],
        out_specs=pl.BlockSpec((1, tt * rpp, A_DH), lambda b, i: (b, i, 0)),
        out_shape=jax.ShapeDtypeStruct((bsz, keep * rpp, A_DH), F32),
        compiler_params=_cparams(("parallel", "parallel")),
        name=f"kv_tail_d{dil}",
    )(a4, a4)
    return out.reshape(bsz, keep, 2, A_HPG, A_DH)


def _kv_rows(a, keep):
    bsz, seq, _ = a.shape
    return a[:, seq - keep:, A_GW:].astype(F32).reshape(bsz, keep, 2, A_HPG, A_DH)


def _trunk(x, mem_kv, s0, chunk, caches, rel_bias, lw, *, tm, act_dtype, cross_tm, cross_nb):
    (g_mix_pre, g_mix_post, w_in, w_f2, b_f2, gla_norm, w_pa, w_pb, w_out,
     g_mem_pre, g_mem_post, w_mq, w_mo, g_ffn_pre, g_ffn_post, w_gate, w_up, w_down) = lw
    bsz, seq, d = x.shape
    x2 = x.reshape(bsz * seq, d)
    if caches is None:
        dils = tuple(dil for _, dil in A_GROUPS)
        *a_g, pg, ob, s_new = _in_proj(x2, g_mix_pre, w_in, bsz, seq, tm, dils, act_dtype,
                                       gla=(w_f2, b_f2, gla_norm, s0, chunk))
        oas, lses, wins = [], [], []
        for gi, (window, dil) in enumerate(A_GROUPS):
            a4 = a_g[gi].reshape(bsz, dil, seq // dil, P_G)
            o, l = _dil_prompt(a4, _prompt_bias_row(rel_bias, gi, dil), gi, dil, bsz, seq, A_NK * A_GROUPS[-1][1])
            oas.append(o)
            lses.append(l)
            keep = min(window, seq)
            wins.append(_kv_tail(a4, keep, 512))
    else:
        *a_g, pb, pg = _in_proj(x2, g_mix_pre, w_in, bsz, seq, tm, (1,) * len(A_GROUPS), act_dtype)
        a_s = [a.reshape(bsz, seq, P_G) for a in a_g]
        res = _dil_sample(a_s, caches, rel_bias)
        oas, lses = res[:3], res[3:]
        wins = [_kv_rows(a, seq) for a in a_s]
        ob, s_new = _gla(pb, w_f2, b_f2, gla_norm, s0, bsz, seq, chunk, seq)
    x2 = _mix_out(oas, lses, ob, pg, x2, w_pa, w_pb, w_out, g_mix_post, min(2 * tm, bsz * seq))
    x2 = _cross(x2, mem_kv, w_mq, w_mo, g_mem_pre, g_mem_post, cross_tm, cross_nb, seq)
    x2 = _ffn(x2, w_gate, w_up, w_down, g_ffn_pre, g_ffn_post, min(2 * tm, bsz * seq))
    return x2.reshape(bsz, seq, d), wins, s_new


def kernel(x_prompt, x_sample, cache_win1_kv, cache_win2_kv, cache_win3_kv, state_gla, cache_mem_kv, mem_prompt, rel_bias, norm_mix_pre, norm_mix_post, w_in, w_f2, b_f2, gla_norm, w_proj_a, w_proj_b, w_out, norm_memtok, w_mk, w_mv, norm_mem_pre, norm_mem_post, w_mq, w_mo, norm_ffn_pre, norm_ffn_post, w_ffn_gate, w_ffn_up, w_ffn_down):
    depth = w_in.shape[0]
    bsz, seq, d = x_prompt.shape
    dbs, dseq, _ = x_sample.shape
    mem_len = mem_prompt.shape[1]
    yp, ys = x_prompt, x_sample
    outs = [[] for _ in range(9)]
    for l in range(depth):
        wl = w_in[l]
        f0 = 3 * A_WIDTH + P_FB_OFF
        w_in_l = (wl[:, :f0].astype(BF16), jnp.pad(wl[:, f0:f0 + B_RANK].astype(BF16), ((0, 0), (0, LANES - B_RANK))),
                  wl[:, f0 + B_RANK:].astype(BF16))
        w_f2_l = jnp.pad(w_f2[l], ((0, LANES - B_RANK), (0, 0))).astype(BF16)
        lw = (_row(norm_mix_pre[l]), _row(norm_mix_post[l]), w_in_l, w_f2_l, _row(b_f2[l]), _row(gla_norm[l]),
              w_proj_a[l].astype(BF16), w_proj_b[l].astype(BF16), w_out[l].astype(BF16),
              _row(norm_mem_pre[l]), _row(norm_mem_post[l]), w_mq[l].astype(BF16), w_mo[l].astype(BF16),
              _row(norm_ffn_pre[l]), _row(norm_ffn_post[l]),
              w_ffn_gate[l].astype(BF16), w_ffn_up[l].astype(BF16), w_ffn_down[l].astype(BF16))
        w_mkv = jnp.concatenate([w_mk[l], w_mv[l]], axis=1).astype(BF16)
        mem_kv_p = _norm_matmul(mem_prompt.reshape(bsz * mem_len, d), _row(norm_memtok[l]), w_mkv,
                                min(256, bsz * mem_len)).reshape(bsz, mem_len, 2 * d)
        s0 = jnp.zeros((bsz, B_HEADS, B_DK, B_DV), F32)
        tm_p = min(512, seq)
        yp, win_p, gla_p = _trunk(yp, mem_kv_p, s0, min(B_CHUNK, seq), None, rel_bias, lw,
                                  tm=tm_p, act_dtype=BF16, cross_tm=tm_p, cross_nb=1)
        caches = (cache_win1_kv[l], cache_win2_kv[l], cache_win3_kv[l])
        cross_nb = min(4, dbs)
        ys, win_s, gla_s = _trunk(ys, cache_mem_kv[l], state_gla[l], dseq, caches,
                                  rel_bias, lw, tm=dbs * dseq, act_dtype=F32,
                                  cross_tm=cross_nb * dseq, cross_nb=cross_nb)
        for lst, v in zip(outs, (*win_p, gla_p, mem_kv_p.reshape(bsz, mem_len, 2, M_HEADS, d // M_HEADS), *win_s, gla_s)):
            lst.append(v)
    return (yp, ys, *[jnp.stack(o) for o in outs])
```

```python
import functools
import math

import numpy as np
import jax
import jax.numpy as jnp
from jax import lax
from jax.experimental import pallas as pl
from jax.experimental.pallas import tpu as pltpu

F32 = jnp.float32
BF16 = jnp.bfloat16

A_GROUPS = ((128, 1), (512, 4), (2048, 16))
A_HPG = 4
A_DH = 128
A_GW = A_HPG * A_DH
A_WIDTH = A_GW * len(A_GROUPS)
A_NK = 128
REL_BUCKETS = 32
REL_MAX_DIST = 2048
B_HEADS = 4
B_DK = 128
B_DV = 256
B_QK = B_HEADS * B_DK
B_V = B_HEADS * B_DV
B_RANK = 16
B_GATE_NORM = 16.0
B_CHUNK = 64
M_HEADS = 4
EPS = 1e-6
NEG_INF = -1e30

LANES = 128
SUBLANES = 8
VMEM_LIMIT_BYTES = 56 * 1024 * 1024

P_G = 3 * A_GW
P_B = 2 * B_QK + 2 * B_V + LANES
P_FB_OFF = 2 * B_QK + 2 * B_V


def _cparams(sem):
    return pltpu.CompilerParams(dimension_semantics=sem, vmem_limit_bytes=VMEM_LIMIT_BYTES)


def _resident(shape):
    nd = len(shape)
    return pl.BlockSpec(shape, lambda *_: (0,) * nd, pipeline_mode=pl.Buffered(1))


def _rms(x, g):
    return x * lax.rsqrt(jnp.mean(x * x, axis=-1, keepdims=True) + EPS) * g


def _dot(a, b):
    return jnp.dot(a, b, preferred_element_type=F32)


def _dot_nt(a, b):
    return lax.dot_general(a, b, (((1,), (1,)), ((), ())), preferred_element_type=F32)


def _dot_tn(a, b):
    return lax.dot_general(a, b, (((0,), (0,)), ((), ())), preferred_element_type=F32)


def _strided(start, size, stride):
    return pl.ds(start, size) if stride == 1 else pl.ds(start, size, stride=stride)


def _in_proj_kernel(x_ref, g_ref, w_ref, wfb_ref, wgt_ref, *refs, dils, chunk, gla):
    ng = len(dils)
    nperm = sum(d > 1 for d in dils)
    if gla:
        (wf_ref, bf_ref, gn_ref, s0_ref), refs = refs[:4], refs[4:]
        a_refs, (pg_ref, ob_ref, sf_ref), scr = refs[:ng], refs[ng:ng + 3], refs[ng + 3:]
        pb_ref, st_scr = scr[2 + nperm], scr[3 + nperm]
        gla_chunk, tps = gla
        t = pl.program_id(0) % tps

        @pl.when(t == 0)
        def _():
            st_scr[...] = s0_ref[0]
    else:
        a_refs, (pb_ref, pg_ref), scr = refs[:ng], refs[ng:ng + 2], refs[ng + 2:]
    h_scr, hs_scr = scr[0], scr[1]
    perm_scr = dict(zip([d for d in dils if d > 1], scr[2:2 + nperm]))
    tm, dm = x_ref.shape
    h = _rms(x_ref[...], g_ref[...])
    h_scr[...] = h.astype(BF16)
    if perm_scr:
        for s in range(dm // LANES):
            hs_scr[s] = h[:, s * LANES:(s + 1) * LANES]
        for d, hp in perm_scr.items():
            n = tm // d
            for s in range(dm // LANES):
                for r in range(d):
                    hp[r * n:(r + 1) * n, s * LANES:(s + 1) * LANES] = hs_scr[s, pl.ds(r, n, stride=d), :].astype(BF16)
    base = 3 * A_WIDTH
    for c0 in range(0, P_FB_OFF, chunk):
        pb_ref[:, c0:c0 + chunk] = _dot(h_scr[...], w_ref[:, base + c0:base + c0 + chunk]).astype(pb_ref.dtype)
    pb_ref[:, P_FB_OFF:] = _dot(h_scr[...], wfb_ref[...]).astype(pb_ref.dtype)
    def group_task(gi, a_ref, d, part):
        def run():
            lhs = h_scr if d == 1 else perm_scr[d]
            n = tm // d
            src = part * A_WIDTH + gi * A_GW
            res = _dot(lhs[...], w_ref[:, src:src + A_GW]).astype(a_ref.dtype)
            if d == 1:
                a_ref[:, part * A_GW:(part + 1) * A_GW] = res
            else:
                for r in range(d):
                    a_ref[0, r, :, part * A_GW:(part + 1) * A_GW] = res[r * n:(r + 1) * n]
        return run

    def gate_task(c0):
        def run():
            pg_ref[:, c0:c0 + chunk] = _dot(h_scr[...], wgt_ref[:, c0:c0 + chunk]).astype(pg_ref.dtype)
        return run

    tasks = [group_task(gi, a_ref, d, part) for gi, (a_ref, d) in enumerate(zip(a_refs, dils)) for part in range(3)]
    tasks += [gate_task(c0) for c0 in range(0, pg_ref.shape[1], chunk)]
    if gla:
        share = -(-len(tasks) // B_HEADS)

        def before_head(h):
            for task in tasks[h * share:(h + 1) * share]:
                task()

        _gla_tile(pb_ref.at[:, 0:B_QK], pb_ref.at[:, B_QK:2 * B_QK], pb_ref.at[:, 2 * B_QK:2 * B_QK + B_V],
                  pb_ref.at[:, 2 * B_QK + B_V:P_FB_OFF], pb_ref.at[:, P_FB_OFF:], wf_ref, bf_ref, gn_ref, ob_ref, st_scr,
                  gla_chunk, tm // gla_chunk, before_head)
    else:
        for task in tasks:
            task()
    if gla:
        @pl.when(t == tps - 1)
        def _():
            sf_ref[0] = st_scr[...]


def _in_proj(x, g, ws, bsz, seq, tm, dils, out_dtype, gla=None):
    m, dm = x.shape
    tps = seq // tm
    w, wfb, wgt = ws
    pg_w = wgt.shape[1]
    row = lambda i: (i, 0)
    out_specs, out_shape = [], []
    for d in dils:
        if d == 1:
            out_specs.append(pl.BlockSpec((tm, P_G), row))
            out_shape.append(jax.ShapeDtypeStruct((m, P_G), out_dtype))
        else:
            out_specs.append(pl.BlockSpec((1, d, tm // d, P_G), lambda i: (i // tps, 0, i % tps, 0)))
            out_shape.append(jax.ShapeDtypeStruct((bsz, d, seq // d, P_G), out_dtype))
    perms = [d for d in dils if d > 1]
    scratch = [pltpu.VMEM((tm, dm), BF16), pltpu.VMEM((dm // LANES, tm, LANES) if perms else (1, SUBLANES, LANES), F32)]
    scratch += [pltpu.VMEM((tm, dm), BF16) for _ in perms]
    in_specs = [pl.BlockSpec((tm, dm), row), _resident(g.shape), _resident(w.shape), _resident(wfb.shape),
                _resident(wgt.shape)]
    args = [x, g, w, wfb, wgt]
    if gla is None:
        widths, cfg, sem = (P_B, pg_w), None, ("parallel",)
    else:
        wf, bf, gn, s0, gla_chunk = gla
        state_spec = pl.BlockSpec((1,) + s0.shape[1:], lambda i: (i // tps, 0, 0, 0))
        in_specs += [_resident(wf.shape), _resident(bf.shape), _resident(gn.shape), state_spec]
        args += [wf, bf, gn, s0]
        scratch += [pltpu.VMEM((tm, P_B), out_dtype), pltpu.VMEM(s0.shape[1:], F32)]
        widths, cfg, sem = (pg_w, B_V), (gla_chunk, tps), ("arbitrary",)
    for n in widths:
        out_specs.append(pl.BlockSpec((tm, n), row))
        out_shape.append(jax.ShapeDtypeStruct((m, n), out_dtype))
    if gla is not None:
        out_specs.append(state_spec)
        out_shape.append(jax.ShapeDtypeStruct(s0.shape, F32))
    return pl.pallas_call(
        functools.partial(_in_proj_kernel, dils=tuple(dils), chunk=512, gla=cfg),
        grid=(m // tm,),
        in_specs=in_specs,
        out_specs=out_specs,
        out_shape=out_shape,
        scratch_shapes=scratch,
        compiler_params=_cparams(sem),
        name="in_proj" if gla is None else "in_proj_gla",
    )(*args)


def _norm_matmul_kernel(x_ref, g_ref, w_ref, o_ref):
    o_ref[...] = _dot(_rms(x_ref[...], g_ref[...]).astype(BF16), w_ref[...]).astype(o_ref.dtype)


def _norm_matmul(x, g, w, tm):
    m, d = x.shape
    n = w.shape[1]
    return pl.pallas_call(
        _norm_matmul_kernel,
        grid=(m // tm,),
        in_specs=[pl.BlockSpec((tm, d), lambda i: (i, 0)), _resident(g.shape), _resident(w.shape)],
        out_specs=pl.BlockSpec((tm, n), lambda i: (i, 0)),
        out_shape=jax.ShapeDtypeStruct((m, n), F32),
        compiler_params=_cparams(("parallel",)),
        name="norm_matmul",
    )(x, g, w)


def _t5_bucket(n):
    max_exact = REL_BUCKETS // 2
    nf = jnp.maximum(n, 1).astype(F32)
    large = max_exact + (jnp.log(nf / max_exact) / math.log(REL_MAX_DIST / max_exact)
                         * (REL_BUCKETS - max_exact)).astype(jnp.int32)
    return jnp.where(n < max_exact, n, jnp.minimum(large, REL_BUCKETS - 1))


def _bias_by_steps(rel_bias, gi, dil):
    steps = jnp.arange(A_NK + 1, dtype=jnp.int32)
    return rel_bias[:, gi * A_HPG:(gi + 1) * A_HPG][_t5_bucket(steps * dil)].T.astype(F32)


def _dil_prompt_kernel(q_ref, kc_ref, kp_ref, vc_ref, vp_ref, u_ref, o_ref, l_ref, bias_scr, of_scr, lf_scr,
                       *, gi, dil, nsub):
    first = pl.program_id(1) == 0
    for h in range(A_HPG):
        bias_scr[h] = pltpu.roll(jnp.broadcast_to(u_ref[h:h + 1, :], (A_NK, 2 * A_NK)), 0, 1, stride=1, stride_axis=0)
    scale = A_DH ** -0.5
    col = lax.broadcasted_iota(jnp.int32, (A_NK, 2 * A_NK), 1)
    lane = lax.broadcasted_iota(jnp.int32, (A_NK, LANES), 1)
    for r in range(dil):
        for s in range(nsub):
            rows = slice(s * A_NK, (s + 1) * A_NK)
            dst = _strided(s * A_NK * dil + r, A_NK, dil)
            lt = jnp.zeros((A_NK, LANES), F32)
            for h in range(A_HPG):
                cols = slice(h * A_DH, (h + 1) * A_DH)
                if s == 0:
                    kw = jnp.concatenate([kp_ref[0, r, :, cols], kc_ref[0, r, rows, cols]], axis=0)
                    vw = jnp.concatenate([vp_ref[0, r, :, cols], vc_ref[0, r, rows, cols]], axis=0)
                else:
                    kw = kc_ref[0, r, (s - 1) * A_NK:(s + 1) * A_NK, cols]
                    vw = vc_ref[0, r, (s - 1) * A_NK:(s + 1) * A_NK, cols]
                sc = _dot_nt(q_ref[0, r, rows, cols], kw) * scale + bias_scr[h]
                if s == 0:
                    sc = jnp.where(jnp.logical_and(first, col < A_NK), NEG_INF, sc)
                mx = jnp.max(sc, axis=-1, keepdims=True)
                p = jnp.exp(sc - mx)
                den = jnp.sum(p, axis=-1, keepdims=True)
                o = _dot(p.astype(BF16), vw) / den
                if dil == 1:
                    o_ref[0, rows, cols] = o.astype(o_ref.dtype)
                else:
                    of_scr[h, dst, :] = o
                lt = jnp.where(lane == gi * A_HPG + h, mx + jnp.log(den), lt)
            if dil == 1:
                l_ref[0, rows, :] = lt
            else:
                lf_scr[dst, :] = lt
    if dil > 1:
        for h in range(A_HPG):
            o_ref[0, :, h * A_DH:(h + 1) * A_DH] = of_scr[h].astype(o_ref.dtype)
        l_ref[0] = lf_scr[...]


def _dil_prompt(a, u, gi, dil, bsz, seq, tt):
    m = seq // dil
    tt = min(tt, seq)
    mt = tt // dil
    nsub = mt // A_NK
    assert mt % A_NK == 0 and seq % tt == 0

    def cur(cb):
        return pl.BlockSpec((1, dil, mt, A_GW), lambda b, i: (b, 0, i, cb))

    def prev(cb):
        return pl.BlockSpec((1, dil, A_NK, A_GW), lambda b, i: (b, 0, jnp.maximum(i * nsub - 1, 0), cb))

    st = tt if dil > 1 else SUBLANES
    o, lse = pl.pallas_call(
        functools.partial(_dil_prompt_kernel, gi=gi, dil=dil, nsub=nsub),
        grid=(bsz, seq // tt),
        in_specs=[cur(0), cur(1), prev(1), cur(2), prev(2), _resident(u.shape)],
        out_specs=[pl.BlockSpec((1, tt, A_GW), lambda b, i: (b, i, 0)),
                   pl.BlockSpec((1, tt, LANES), lambda b, i: (b, i, 0))],
        out_shape=[jax.ShapeDtypeStruct((bsz, seq, A_GW), a.dtype), jax.ShapeDtypeStruct((bsz, seq, LANES), F32)],
        scratch_shapes=[pltpu.VMEM((A_HPG, A_NK, 2 * A_NK), F32), pltpu.VMEM((A_HPG, st, A_DH), F32),
                        pltpu.VMEM((st, LANES), F32)],
        compiler_params=_cparams(("parallel", "arbitrary")),
        name=f"dil_prompt_g{gi}",
    )(a, a, a, a, a, u)
    return o.reshape(bsz * seq, A_GW), lse.reshape(bsz * seq, LANES)


def _prompt_bias_row(rel_bias, gi, dil):
    tab = _bias_by_steps(rel_bias, gi, dil)
    return jnp.concatenate([tab[:, ::-1], jnp.full((A_HPG, A_NK - 1), NEG_INF, F32)], axis=1)


def _dil_sample_kernel(a1_ref, a2_ref, a3_ref, c1_ref, c2_ref, c3_ref, bc1_ref, bc2_ref, bc3_ref, bn_ref,
                       o1_ref, o2_ref, o3_ref, l1_ref, l2_ref, l3_ref):
    scale = A_DH ** -0.5
    nq = a1_ref.shape[1]
    groups = ((a1_ref, c1_ref, bc1_ref, o1_ref, l1_ref), (a2_ref, c2_ref, bc2_ref, o2_ref, l2_ref),
              (a3_ref, c3_ref, bc3_ref, o3_ref, l3_ref))
    for gi, (a_ref, c_ref, bc_ref, o_ref, l_ref) in enumerate(groups):
        rows = c_ref[0]
        if rows.ndim == 3:
            rows = rows.reshape(rows.shape[0] * rows.shape[1], rows.shape[2])
        rows = rows.astype(BF16)
        qs = [a_ref[0, :, h * A_DH:(h + 1) * A_DH] for h in range(A_HPG)]
        q_all = jnp.concatenate(qs, axis=0).astype(BF16)
        sc = _dot_nt(q_all, rows) * scale + bc_ref[...]
        scn = [_dot_nt(qs[h].astype(BF16), a_ref[0, :, A_GW + h * A_DH:A_GW + (h + 1) * A_DH].astype(BF16)) * scale
               + bn_ref[gi, h] for h in range(A_HPG)]
        scn = jnp.concatenate(scn, axis=0)
        mx = jnp.maximum(jnp.max(sc, axis=-1, keepdims=True), jnp.max(scn, axis=-1, keepdims=True))
        p = jnp.exp(sc - mx)
        pn = jnp.exp(scn - mx)
        den = jnp.sum(p, axis=-1, keepdims=True) + jnp.sum(pn, axis=-1, keepdims=True)
        acc = _dot(pltpu.roll(p, A_HPG, 1).astype(BF16), rows)
        lse = mx + jnp.log(den)
        lane = lax.broadcasted_iota(jnp.int32, (nq, LANES), 1)
        lt = jnp.zeros((nq, LANES), F32)
        for h in range(A_HPG):
            hr = slice(h * nq, (h + 1) * nq)
            vnew = a_ref[0, :, 2 * A_GW + h * A_DH:2 * A_GW + (h + 1) * A_DH].astype(BF16)
            o = (acc[hr] + _dot(pn[hr].astype(BF16), vnew)) / den[hr]
            o_ref[0, :, h * A_DH:(h + 1) * A_DH] = o.astype(o_ref.dtype)
            lt = jnp.where(lane == gi * A_HPG + h, lse[hr], lt)
        l_ref[0] = lt


def _toeplitz(v, n, width):
    lead, length = v.shape[:-1], v.shape[-1]
    t = jnp.tile(v, (1,) * len(lead) + (n,))[..., :n * (length - 1)]
    return t.reshape(lead + (n, length - 1))[..., :width]


def _sample_biases(rel_bias, nq, nres):
    cache, new = [], []
    for gi, (window, dil) in enumerate(A_GROUPS):
        tab = _bias_by_steps(rel_bias, gi, dil)
        length = window + nq
        rep = jnp.repeat(tab, dil, axis=1)
        rep = jnp.where((jnp.arange(rep.shape[1]) % dil == 0)[None], rep, NEG_INF)
        by = jnp.concatenate([rep, jnp.full((A_HPG, max(length - rep.shape[1], 0)), NEG_INF, F32)], axis=1)[:, :length]
        bc = _toeplitz(jnp.roll(by[:, ::-1], -(nq - 1), axis=1), nq, window)
        bn = _toeplitz(jnp.concatenate([by[:, :1], jnp.full((A_HPG, nq), NEG_INF, F32), by[:, nq - 1:0:-1]], axis=1),
                       nq, nq)
        own = (jnp.arange(2 * A_HPG)[None, :] == jnp.arange(A_HPG)[:, None])
        full = jnp.where(own[:, None, None, :], bc[:, :, :, None], NEG_INF)
        if gi == len(A_GROUPS) - 1:
            full = full.reshape(A_HPG, nq, window // dil, dil, 2 * A_HPG)[:, :, :, :nres]
        cache.append(full.reshape(A_HPG * nq, -1))
        new.append(bn)
    return cache, jnp.stack(new)


def _dil_sample(a_s, caches, rel_bias):
    bsz, nq, _ = a_s[0].shape
    w3, dil3 = caches[2].shape[1], A_GROUPS[2][1]
    nres = min(dil3, nq)
    assert w3 % dil3 == 0 and nq <= dil3, "new tokens must fall in distinct residue classes of the widest dilation"
    bcs, bn = _sample_biases(rel_bias, nq, nres)
    rpp = 2 * A_HPG
    c1 = caches[0].reshape(bsz, caches[0].shape[1] * rpp, A_DH)
    c2 = caches[1].reshape(bsz, caches[1].shape[1] * rpp, A_DH)
    c3 = caches[2].reshape(bsz, w3 // dil3, dil3 * rpp, A_DH)
    row = lambda b: (b, 0, 0)
    outs = pl.pallas_call(
        _dil_sample_kernel,
        grid=(bsz,),
        in_specs=[pl.BlockSpec((1, nq, P_G), row)] * 3
        + [pl.BlockSpec((1,) + c1.shape[1:], row), pl.BlockSpec((1,) + c2.shape[1:], row),
           pl.BlockSpec((1, w3 // dil3, nres * rpp, A_DH), lambda b: (b, 0, 0, 0)),
           _resident(bcs[0].shape), _resident(bcs[1].shape), _resident(bcs[2].shape), _resident(bn.shape)],
        out_specs=[pl.BlockSpec((1, nq, A_GW), row)] * 3 + [pl.BlockSpec((1, nq, LANES), row)] * 3,
        out_shape=[jax.ShapeDtypeStruct((bsz, nq, A_GW), F32)] * 3 + [jax.ShapeDtypeStruct((bsz, nq, LANES), F32)] * 3,
        compiler_params=_cparams(("parallel",)),
        name="dil_sample",
    )(*a_s, c1, c2, c3, bcs[0], bcs[1], bcs[2], bn)
    return [o.reshape(bsz * nq, o.shape[-1]) for o in outs]


def _gla_kernel(q_ref, k_ref, v_ref, gb_ref, fb_ref, wf_ref, bf_ref, gn_ref, s0_ref, o_ref, sf_ref, st_scr,
                *, chunk, nchunks):
    t = pl.program_id(1)

    @pl.when(t == 0)
    def _():
        st_scr[...] = s0_ref[0]

    _gla_tile(q_ref.at[0], k_ref.at[0], v_ref.at[0], gb_ref.at[0], fb_ref.at[0], wf_ref, bf_ref, gn_ref, o_ref.at[0],
              st_scr, chunk, nchunks)

    @pl.when(t == pl.num_programs(1) - 1)
    def _():
        sf_ref[0] = st_scr[...]


def _gla_tile(q_ref, k_ref, v_ref, gb_ref, fb_ref, wf_ref, bf_ref, gn_ref, o_ref, st_scr, chunk, nchunks,
              before_head=None):
    tt = chunk * nchunks
    mid = chunk // 2
    z = _dot(fb_ref[...].astype(BF16), wf_ref[...]) + bf_ref[...]
    lf_all = (jnp.minimum(z, 0.0) - jnp.log(1.0 + jnp.exp(-jnp.abs(z)))) * (1.0 / B_GATE_NORM)
    blk = tt
    ri = lax.broadcasted_iota(jnp.int32, (blk, blk), 0)
    ci = lax.broadcasted_iota(jnp.int32, (blk, blk), 1)
    causal = jnp.logical_and(ri >= ci, ri // chunk == ci // chunk)
    pos = lax.broadcasted_iota(jnp.int32, (tt, B_DK), 0) % chunk
    ones_col = jnp.ones((chunk, 1), F32)
    hs = slice(0, B_DK)
    for h in range(B_HEADS):
        if before_head is not None:
            before_head(h)
        hcols = slice(h * B_DK, (h + 1) * B_DK)
        vs = slice(h * B_DV, (h + 1) * B_DV)
        lf = lf_all[:, hcols]
        b = lf
        sh = 1
        while sh < chunk:
            b = b + jnp.where(pos >= sh, pltpu.roll(b, sh, 0), 0.0)
            sh *= 2
        b3 =b.reshape(nchunks, chunk, B_DK)
        bm = jnp.broadcast_to(b3[:, mid:mid + 1, :], b3.shape).reshape(tt, B_DK)
        blr = b3[:, chunk - 1, :]
        bl = jnp.broadcast_to(b3[:, chunk - 1:chunk, :], b3.shape).reshape(tt, B_DK)
        q = q_ref[:, hcols].astype(F32) * (B_DK ** -0.5)
        k = k_ref[:, hcols].astype(F32)
        qe = (q * jnp.exp(b)).astype(BF16)
        qm = (q * jnp.exp(b - bm)).astype(BF16)
        km = (k * jnp.exp(bm - b)).astype(BF16)
        kl = (k * jnp.exp(bl - b)).astype(BF16)
        dec_t = jnp.exp(blr).T if nchunks >= SUBLANES else None
        v = v_ref[:, vs].astype(BF16)
        chunks = [slice(c * chunk, (c + 1) * chunk) for c in range(nchunks)]
        us = [_dot_tn(kl[rows, hs], v[rows]) for rows in chunks]
        o_parts = []
        for r0 in range(0, tt, blk):
            br = slice(r0, r0 + blk)
            a = jnp.where(causal, _dot_nt(qm[br, hs], km[br, hs]), 0.0)
            o_parts.append(_dot(a.astype(BF16), v[br]))
        o_intra = o_parts[0] if len(o_parts) == 1 else jnp.concatenate(o_parts, axis=0)
        gbv = gb_ref[:, vs].astype(F32)
        gate = gbv * jax.nn.sigmoid(gbv)
        st = st_scr[h]
        sts = []
        for c, rows in enumerate(chunks):
            sts.append(st.astype(BF16))
            if dec_t is not None:
                dcol = dec_t[hs, c:c + 1]
            else:
                dcol = jnp.exp(_dot_tn(lf[rows, hs], ones_col))
            st = st * dcol + us[c]
        for c, rows in enumerate(chunks):
            o = o_intra[rows] + _dot(qe[rows, hs], sts[c])
            o_ref[rows, vs] = (_rms(o, gn_ref[...]) * gate[rows]).astype(o_ref.dtype)
        st_scr[h] = st


def _gla(pb, wf, bf, gn, s0, bsz, seq, chunk, tt):
    pb3 = pb.reshape(bsz, seq, P_B)
    tt = min(tt, seq)
    o, sf = pl.pallas_call(
        functools.partial(_gla_kernel, chunk=chunk, nchunks=tt // chunk),
        grid=(bsz, seq // tt),
        in_specs=[pl.BlockSpec((1, tt, B_QK), lambda b, t: (b, t, 0)),
                  pl.BlockSpec((1, tt, B_QK), lambda b, t: (b, t, 1)),
                  pl.BlockSpec((1, tt, B_V), lambda b, t: (b, t, 2 * B_QK // B_V)),
                  pl.BlockSpec((1, tt, B_V), lambda b, t: (b, t, 2 * B_QK // B_V + 1)),
                  pl.BlockSpec((1, tt, LANES), lambda b, t: (b, t, P_FB_OFF // LANES)),
                  _resident(wf.shape), _resident(bf.shape), _resident(gn.shape),
                  pl.BlockSpec((1, B_HEADS, B_DK, B_DV), lambda b, t: (b, 0, 0, 0))],
        out_specs=[pl.BlockSpec((1, tt, B_V), lambda b, t: (b, t, 0)),
                   pl.BlockSpec((1, B_HEADS, B_DK, B_DV), lambda b, t: (b, 0, 0, 0))],
        out_shape=[jax.ShapeDtypeStruct((bsz, seq, B_V), pb.dtype),
                   jax.ShapeDtypeStruct((bsz, B_HEADS, B_DK, B_DV), F32)],
        scratch_shapes=[pltpu.VMEM((B_HEADS, B_DK, B_DV), F32)],
        compiler_params=_cparams(("parallel", "arbitrary")),
        name="gla",
    )(pb3, pb3, pb3, pb3, pb3, wf, bf, gn, s0)
    return o.reshape(bsz * seq, B_V), sf


def _mix_out_kernel(o1_ref, o2_ref, o3_ref, l1_ref, l2_ref, l3_ref, ob_ref, ga_ref, gbt_ref, x_ref,
                    ex_ref, wa_ref, wb_ref, wo_ref, gp_ref, y_ref, *, nsplit):
    n = x_ref.shape[0] // nsplit
    subs = []
    for i in range(nsplit):
        rows = slice(i * n, (i + 1) * n)
        subs.append(_mix_out_rows(*(r.at[rows] for r in (o1_ref, o2_ref, o3_ref, l1_ref, l2_ref, l3_ref, ob_ref, ga_ref,
                                                         gbt_ref, x_ref)),
                                  ex_ref, wa_ref, wb_ref, wo_ref, gp_ref, y_ref.at[rows]))
    live = []
    while subs or live:
        if subs:
            live.append(subs.pop(0))
        live = [g for g in live if next(g, "done") != "done"]


def _mix_out_rows(o1_ref, o2_ref, o3_ref, l1_ref, l2_ref, l3_ref, ob_ref, ga_ref, gbt_ref, x_ref,
                  ex_ref, wa_ref, wb_ref, wo_ref, gp_ref, y_ref):
    yb = _dot(ob_ref[...].astype(BF16), wb_ref[...])
    yield
    ng = len(A_GROUPS)
    ls = l1_ref[...] + l2_ref[...] + l3_ref[...]
    lane = lax.broadcasted_iota(jnp.int32, ls.shape, 1)

    def over_groups(v, op):
        red = v
        for g in range(1, ng):
            red = op(red, pltpu.roll(v, LANES - g * A_HPG, 1))
        out = red
        for g in range(1, ng):
            out = jnp.where(lane >= g * A_HPG, pltpu.roll(red, g * A_HPG, 1), out)
        return out

    e = jnp.exp(ls - over_groups(ls, jnp.maximum))
    w = jnp.where(lane < ng * A_HPG, e / over_groups(e, jnp.add), 0.0)
    yield
    wf = _dot(w.astype(BF16), ex_ref[...])
    yield
    comb = (wf[:, :A_GW] * o1_ref[...].astype(F32) + wf[:, A_GW:2 * A_GW] * o2_ref[...].astype(F32)
            + wf[:, 2 * A_GW:] * o3_ref[...].astype(F32)).astype(BF16)
    yield
    ya = _dot(comb, wa_ref[...])
    yield
    mix = (jax.nn.sigmoid(ga_ref[...].astype(F32)) * ya + jax.nn.sigmoid(gbt_ref[...].astype(F32)) * yb).astype(BF16)
    yield
    out = _dot(mix, wo_ref[...])
    yield
    y_ref[...] = x_ref[...] + _rms(out, gp_ref[...])


def _mix_out(oas, lses, ob, gates, x, wa, wb, wo, gpost, tm):
    m, d = x.shape
    row = lambda i: (i, 0)
    ex = (jnp.arange(LANES)[:, None] == jnp.arange(A_WIDTH)[None, :] // A_DH).astype(BF16)
    return pl.pallas_call(
        functools.partial(_mix_out_kernel, nsplit=2 if tm % (2 * 2 * SUBLANES) == 0 else 1),
        grid=(m // tm,),
        in_specs=[pl.BlockSpec((tm, A_GW), row)] * 3 + [pl.BlockSpec((tm, LANES), row)] * 3
        + [pl.BlockSpec((tm, B_V), row), pl.BlockSpec((tm, d), row), pl.BlockSpec((tm, d), lambda i: (i, 1)),
           pl.BlockSpec((tm, d), row), _resident(ex.shape), _resident(wa.shape), _resident(wb.shape),
           _resident(wo.shape), _resident(gpost.shape)],
        out_specs=pl.BlockSpec((tm, d), row),
        out_shape=jax.ShapeDtypeStruct((m, d), F32),
        compiler_params=_cparams(("parallel",)),
        name="mix_out",
    )(*oas, *lses, ob, gates, gates, x, ex, wa, wb, wo, gpost)


def _cross_kernel(x_ref, kv_ref, wq_ref, wo_ref, gpre_ref, gpost_ref, y_ref, q_scr, c_scr, *, nb, nsplit):
    tm = x_ref.shape[0]
    if nb == 1:
        n = tm // nsplit
        for i in range(nsplit):
            _cross_rows(x_ref, kv_ref, wq_ref, wo_ref, gpre_ref, gpost_ref, y_ref, q_scr, c_scr,
                        slice(i * n, (i + 1) * n), [(0, slice(i * n, (i + 1) * n))])
    else:
        rpb = tm // nb
        _cross_rows(x_ref, kv_ref, wq_ref, wo_ref, gpre_ref, gpost_ref, y_ref, q_scr, c_scr,
                    slice(0, tm), [(j, slice(j * rpb, (j + 1) * rpb)) for j in range(nb)])


def _cross_rows(x_ref, kv_ref, wq_ref, wo_ref, gpre_ref, gpost_ref, y_ref, q_scr, c_scr, tile, batches):
    x = x_ref[tile]
    d = x.shape[1]
    dh = d // M_HEADS
    q_scr[tile] = _dot(_rms(x, gpre_ref[...]).astype(BF16), wq_ref[...])
    scale = dh ** -0.5
    for j, rows in batches:
        rpb = rows.stop - rows.start
        if len(kv_ref.shape) == 5:
            mem = kv_ref.shape[1]
            kall = kv_ref[j, :, 0].reshape(mem * M_HEADS, dh).astype(BF16)
            vall = kv_ref[j, :, 1].reshape(mem * M_HEADS, dh).astype(BF16)
            qall = jnp.concatenate([q_scr[rows, h * dh:(h + 1) * dh] for h in range(M_HEADS)], axis=0).astype(BF16)
            rh = lax.broadcasted_iota(jnp.int32, (M_HEADS * rpb, mem * M_HEADS), 0) // rpb
            ch = lax.broadcasted_iota(jnp.int32, (M_HEADS * rpb, mem * M_HEADS), 1) % M_HEADS
            sc = jnp.where(rh == ch, _dot_nt(qall, kall) * scale, NEG_INF)
            p = jnp.exp(sc - jnp.max(sc, axis=-1, keepdims=True))
            ctx = _dot(p.astype(BF16), vall) / jnp.sum(p, axis=-1, keepdims=True)
            for h in range(M_HEADS):
                c_scr[rows, h * dh:(h + 1) * dh] = ctx[h * rpb:(h + 1) * rpb]
            continue
        for h in range(M_HEADS):
            cols = slice(h * dh, (h + 1) * dh)
            k = kv_ref[j, :, cols].astype(BF16)
            v = kv_ref[j, :, d + cols.start:d + cols.stop].astype(BF16)
            sc = _dot_nt(q_scr[rows, cols].astype(BF16), k) * scale
            p = jnp.exp(sc - jnp.max(sc, axis=-1, keepdims=True))
            den = jnp.sum(p, axis=-1, keepdims=True)
            c_scr[rows, cols] = _dot(p.astype(BF16), v) / den
    out = _dot(c_scr[tile].astype(BF16), wo_ref[...])
    y_ref[tile] = x + _rms(out, gpost_ref[...])


def _cross(x, kv, wq, wo, gpre, gpost, tm, nb, rows_per_batch):
    m, d = x.shape
    tiles_per_kv = max(rows_per_batch // tm, 1)
    kv_tail = (0,) * (kv.ndim - 1)
    return pl.pallas_call(
        functools.partial(_cross_kernel, nb=nb, nsplit=1),
        grid=(m // tm,),
        in_specs=[pl.BlockSpec((tm, d), lambda i: (i, 0)),
                  pl.BlockSpec((nb,) + kv.shape[1:], lambda i: (i // tiles_per_kv,) + kv_tail),
                  _resident(wq.shape), _resident(wo.shape), _resident(gpre.shape), _resident(gpost.shape)],
        out_specs=pl.BlockSpec((tm, d), lambda i: (i, 0)),
        out_shape=jax.ShapeDtypeStruct((m, d), F32),
        scratch_shapes=[pltpu.VMEM((tm, d), F32), pltpu.VMEM((tm, d), F32)],
        compiler_params=_cparams(("parallel",)),
        name="cross_attn",
    )(x, kv, wq, wo, gpre, gpost)


def _ffn_kernel(x_ref, wg_ref, wu_ref, wd_ref, gpre_ref, gpost_ref, y_ref, h_scr, *, chunk, nsplit):
    n = x_ref.shape[0] // nsplit
    for i in range(nsplit):
        rows = slice(i * n, (i + 1) * n)
        x = x_ref[rows]
        h_scr[rows] = _rms(x, gpre_ref[...]).astype(BF16)
        acc = jnp.zeros(x.shape, F32)
        for c0 in range(0, wg_ref.shape[1], chunk):
            g = _dot(h_scr[rows], wg_ref[:, c0:c0 + chunk])
            u = _dot(h_scr[rows], wu_ref[:, c0:c0 + chunk])
            acc = acc + _dot((g * jax.nn.sigmoid(g) * u).astype(BF16), wd_ref[c0:c0 + chunk, :])
        y_ref[rows] = x + _rms(acc, gpost_ref[...])


def _ffn(x, wg, wu, wd, gpre, gpost, tm):
    m, d = x.shape
    return pl.pallas_call(
        functools.partial(_ffn_kernel, chunk=256, nsplit=1),
        grid=(m // tm,),
        in_specs=[pl.BlockSpec((tm, d), lambda i: (i, 0)), _resident(wg.shape), _resident(wu.shape),
                  _resident(wd.shape), _resident(gpre.shape), _resident(gpost.shape)],
        out_specs=pl.BlockSpec((tm, d), lambda i: (i, 0)),
        out_shape=jax.ShapeDtypeStruct((m, d), F32),
        scratch_shapes=[pltpu.VMEM((tm, d), BF16)],
        compiler_params=_cparams(("parallel",)),
        name="ffn",
    )(x, wg, wu, wd, gpre, gpost)


def _row(v):
    return v.reshape(1, -1).astype(F32)


def _kv_tail_kernel(k_ref, v_ref, o_ref, *, dil):
    n = k_ref.shape[2]
    rpp = 2 * A_HPG
    for part, ref in enumerate((k_ref, v_ref)):
        for h in range(A_HPG):
            for r in range(dil):
                piece = ref[0, r, :, h * A_DH:(h + 1) * A_DH].astype(F32)
                o_ref[0, _strided(r * rpp + part * A_HPG + h, n, rpp * dil), :] = piece


def _kv_tail(a4, keep, tt):
    bsz, dil, m, _ = a4.shape
    tt = min(tt, keep)
    n = tt // dil
    first = (m - keep // dil) // n
    rpp = 2 * A_HPG

    def src(cb):
        return pl.BlockSpec((1, dil, n, A_GW), lambda b, i: (b, 0, first + i, cb))

    out = pl.pallas_call(
        functools.partial(_kv_tail_kernel, dil=dil),
        grid=(bsz, keep // tt),
        in_specs=[src(1), src(2)],
        out_specs=pl.BlockSpec((1, tt * rpp, A_DH), lambda b, i: (b, i, 0)),
        out_shape=jax.ShapeDtypeStruct((bsz, keep * rpp, A_DH), F32),
        compiler_params=_cparams(("parallel", "parallel")),
        name=f"kv_tail_d{dil}",
    )(a4, a4)
    return out.reshape(bsz, keep, 2, A_HPG, A_DH)


def _kv_rows(a, keep):
    bsz, seq, _ = a.shape
    return a[:, seq - keep:, A_GW:].astype(F32).reshape(bsz, keep, 2, A_HPG, A_DH)


def _trunk(x, mem_kv, s0, chunk, caches, rel_bias, lw, *, tm, act_dtype, cross_tm, cross_nb):
    (g_mix_pre, g_mix_post, w_in, w_f2, b_f2, gla_norm, w_pa, w_pb, w_out,
     g_mem_pre, g_mem_post, w_mq, w_mo, g_ffn_pre, g_ffn_post, w_gate, w_up, w_down) = lw
    bsz, seq, d = x.shape
    x2 = x.reshape(bsz * seq, d)
    if caches is None:
        dils = tuple(dil for _, dil in A_GROUPS)
        *a_g, pg, ob, s_new = _in_proj(x2, g_mix_pre, w_in, bsz, seq, tm, dils, act_dtype,
                                       gla=(w_f2, b_f2, gla_norm, s0, chunk))
        oas, lses, wins = [], [], []
        for gi, (window, dil) in enumerate(A_GROUPS):
            a4 = a_g[gi].reshape(bsz, dil, seq // dil, P_G)
            o, l = _dil_prompt(a4, _prompt_bias_row(rel_bias, gi, dil), gi, dil, bsz, seq, A_NK * A_GROUPS[-1][1])
            oas.append(o)
            lses.append(l)
            keep = min(window, seq)
            wins.append(_kv_tail(a4, keep, 512))
    else:
        *a_g, pb, pg = _in_proj(x2, g_mix_pre, w_in, bsz, seq, tm, (1,) * len(A_GROUPS), act_dtype)
        a_s = [a.reshape(bsz, seq, P_G) for a in a_g]
        res = _dil_sample(a_s, caches, rel_bias)
        oas, lses = res[:3], res[3:]
        wins = [_kv_rows(a, seq) for a in a_s]
        ob, s_new = _gla(pb, w_f2, b_f2, gla_norm, s0, bsz, seq, chunk, seq)
    x2 = _mix_out(oas, lses, ob, pg, x2, w_pa, w_pb, w_out, g_mix_post, min(2 * tm, bsz * seq))
    x2 = _cross(x2, mem_kv, w_mq, w_mo, g_mem_pre, g_mem_post, cross_tm, cross_nb, seq)
    x2 = _ffn(x2, w_gate, w_up, w_down, g_ffn_pre, g_ffn_post, min(2 * tm, bsz * seq))
    return x2.reshape(bsz, seq, d), wins, s_new


def kernel(x_prompt, x_sample, cache_win1_kv, cache_win2_kv, cache_win3_kv, state_gla, cache_mem_kv, mem_prompt, rel_bias, norm_mix_pre, norm_mix_post, w_in, w_f2, b_f2, gla_norm, w_proj_a, w_proj_b, w_out, norm_memtok, w_mk, w_mv, norm_mem_pre, norm_mem_post, w_mq, w_mo, norm_ffn_pre, norm_ffn_post, w_ffn_gate, w_ffn_up, w_ffn_down):
    depth = w_in.shape[0]
    bsz, seq, d = x_prompt.shape
    dbs, dseq, _ = x_sample.shape
    mem_len = mem_prompt.shape[1]
    yp, ys = x_prompt, x_sample
    outs = [[] for _ in range(9)]
    for l in range(depth):
        wl = w_in[l]
        f0 = 3 * A_WIDTH + P_FB_OFF
        w_in_l = (wl[:, :f0].astype(BF16), jnp.pad(wl[:, f0:f0 + B_RANK].astype(BF16), ((0, 0), (0, LANES - B_RANK))),
                  wl[:, f0 + B_RANK:].astype(BF16))
        w_f2_l = jnp.pad(w_f2[l], ((0, LANES - B_RANK), (0, 0))).astype(BF16)
        lw = (_row(norm_mix_pre[l]), _row(norm_mix_post[l]), w_in_l, w_f2_l, _row(b_f2[l]), _row(gla_norm[l]),
              w_proj_a[l].astype(BF16), w_proj_b[l].astype(BF16), w_out[l].astype(BF16),
              _row(norm_mem_pre[l]), _row(norm_mem_post[l]), w_mq[l].astype(BF16), w_mo[l].astype(BF16),
              _row(norm_ffn_pre[l]), _row(norm_ffn_post[l]),
              w_ffn_gate[l].astype(BF16), w_ffn_up[l].astype(BF16), w_ffn_down[l].astype(BF16))
        w_mkv = jnp.concatenate([w_mk[l], w_mv[l]], axis=1).astype(BF16)
        mem_kv_p = _norm_matmul(mem_prompt.reshape(bsz * mem_len, d), _row(norm_memtok[l]), w_mkv,
                                min(256, bsz * mem_len)).reshape(bsz, mem_len, 2 * d)
        s0 = jnp.zeros((bsz, B_HEADS, B_DK, B_DV), F32)
        tm_p = min(512, seq)
        yp, win_p, gla_p = _trunk(yp, mem_kv_p, s0, min(B_CHUNK, seq), None, rel_bias, lw,
                                  tm=tm_p, act_dtype=BF16, cross_tm=tm_p, cross_nb=1)
        caches = (cache_win1_kv[l], cache_win2_kv[l], cache_win3_kv[l])
        cross_nb = min(4, dbs)
        ys, win_s, gla_s = _trunk(ys, cache_mem_kv[l], state_gla[l], dseq, caches,
                                  rel_bias, lw, tm=dbs * dseq, act_dtype=F32,
                                  cross_tm=cross_nb * dseq, cross_nb=cross_nb)
        for lst, v in zip(outs, (*win_p, gla_p, mem_kv_p.reshape(bsz, mem_len, 2, M_HEADS, d // M_HEADS), *win_s, gla_s)):
            lst.append(v)
    return (yp, ys, *[jnp.stack(o) for o in outs])
```

```python
import functools
import math

import numpy as np
import jax
import jax.numpy as jnp
from jax import lax
from jax.experimental import pallas as pl
from jax.experimental.pallas import tpu as pltpu

F32 = jnp.float32
BF16 = jnp.bfloat16

A_GROUPS = ((128, 1), (512, 4), (2048, 16))
A_HPG = 4
A_DH = 128
A_GW = A_HPG * A_DH
A_WIDTH = A_GW * len(A_GROUPS)
A_NK = 128
REL_BUCKETS = 32
REL_MAX_DIST = 2048
B_HEADS = 4
B_DK = 128
B_DV = 256
B_QK = B_HEADS * B_DK
B_V = B_HEADS * B_DV
B_RANK = 16
B_GATE_NORM = 16.0
B_CHUNK = 64
M_HEADS = 4
EPS = 1e-6
NEG_INF = -1e30

LANES = 128
SUBLANES = 8
VMEM_LIMIT_BYTES = 56 * 1024 * 1024

P_G = 3 * A_GW
P_B = 2 * B_QK + 2 * B_V + LANES
P_FB_OFF = 2 * B_QK + 2 * B_V


def _cparams(sem):
    return pltpu.CompilerParams(dimension_semantics=sem, vmem_limit_bytes=VMEM_LIMIT_BYTES)


def _resident(shape):
    nd = len(shape)
    return pl.BlockSpec(shape, lambda *_: (0,) * nd, pipeline_mode=pl.Buffered(1))


def _rms(x, g):
    return x * lax.rsqrt(jnp.mean(x * x, axis=-1, keepdims=True) + EPS) * g


def _dot(a, b):
    return jnp.dot(a, b, preferred_element_type=F32)


def _dot_nt(a, b):
    return lax.dot_general(a, b, (((1,), (1,)), ((), ())), preferred_element_type=F32)


def _dot_tn(a, b):
    return lax.dot_general(a, b, (((0,), (0,)), ((), ())), preferred_element_type=F32)


def _strided(start, size, stride):
    return pl.ds(start, size) if stride == 1 else pl.ds(start, size, stride=stride)


def _in_proj_kernel(x_ref, g_ref, w_ref, wfb_ref, wgt_ref, *refs, dils, chunk, gla):
    ng = len(dils)
    nperm = sum(d > 1 for d in dils)
    if gla:
        (wf_ref, bf_ref, gn_ref, s0_ref), refs = refs[:4], refs[4:]
        a_refs, (pg_ref, ob_ref, sf_ref), scr = refs[:ng], refs[ng:ng + 3], refs[ng + 3:]
        pb_ref, st_scr = scr[2 + nperm], scr[3 + nperm]
        gla_chunk, tps = gla
        t = pl.program_id(0) % tps

        @pl.when(t == 0)
        def _():
            st_scr[...] = s0_ref[0]
    else:
        a_refs, (pb_ref, pg_ref), scr = refs[:ng], refs[ng:ng + 2], refs[ng + 2:]
    h_scr, hs_scr = scr[0], scr[1]
    perm_scr = dict(zip([d for d in dils if d > 1], scr[2:2 + nperm]))
    tm, dm = x_ref.shape
    h = _rms(x_ref[...], g_ref[...])
    h_scr[...] = h.astype(BF16)
    if perm_scr:
        for s in range(dm // LANES):
            hs_scr[s] = h[:, s * LANES:(s + 1) * LANES]
        for d, hp in perm_scr.items():
            n = tm // d
            for s in range(dm // LANES):
                for r in range(d):
                    hp[r * n:(r + 1) * n, s * LANES:(s + 1) * LANES] = hs_scr[s, pl.ds(r, n, stride=d), :].astype(BF16)
    base = 3 * A_WIDTH
    for c0 in range(0, P_FB_OFF, chunk):
        pb_ref[:, c0:c0 + chunk] = _dot(h_scr[...], w_ref[:, base + c0:base + c0 + chunk]).astype(pb_ref.dtype)
    pb_ref[:, P_FB_OFF:] = _dot(h_scr[...], wfb_ref[...]).astype(pb_ref.dtype)
    def group_task(gi, a_ref, d, part):
        def run():
            lhs = h_scr if d == 1 else perm_scr[d]
            n = tm // d
            src = part * A_WIDTH + gi * A_GW
            res = _dot(lhs[...], w_ref[:, src:src + A_GW]).astype(a_ref.dtype)
            if d == 1:
                a_ref[:, part * A_GW:(part + 1) * A_GW] = res
            else:
                for r in range(d):
                    a_ref[0, r, :, part * A_GW:(part + 1) * A_GW] = res[r * n:(r + 1) * n]
        return run

    def gate_task(c0):
        def run():
            pg_ref[:, c0:c0 + chunk] = _dot(h_scr[...], wgt_ref[:, c0:c0 + chunk]).astype(pg_ref.dtype)
        return run

    tasks = [group_task(gi, a_ref, d, part) for gi, (a_ref, d) in enumerate(zip(a_refs, dils)) for part in range(3)]
    tasks += [gate_task(c0) for c0 in range(0, pg_ref.shape[1], chunk)]
    if gla:
        share = -(-len(tasks) // B_HEADS)

        def before_head(h):
            for task in tasks[h * share:(h + 1) * share]:
                task()

        _gla_tile(pb_ref.at[:, 0:B_QK], pb_ref.at[:, B_QK:2 * B_QK], pb_ref.at[:, 2 * B_QK:2 * B_QK + B_V],
                  pb_ref.at[:, 2 * B_QK + B_V:P_FB_OFF], pb_ref.at[:, P_FB_OFF:], wf_ref, bf_ref, gn_ref, ob_ref, st_scr,
                  gla_chunk, tm // gla_chunk, before_head)
    else:
        for task in tasks:
            task()
    if gla:
        @pl.when(t == tps - 1)
        def _():
            sf_ref[0] = st_scr[...]


def _in_proj(x, g, ws, bsz, seq, tm, dils, out_dtype, gla=None):
    m, dm = x.shape
    tps = seq // tm
    w, wfb, wgt = ws
    pg_w = wgt.shape[1]
    row = lambda i: (i, 0)
    out_specs, out_shape = [], []
    for d in dils:
        if d == 1:
            out_specs.append(pl.BlockSpec((tm, P_G), row))
            out_shape.append(jax.ShapeDtypeStruct((m, P_G), out_dtype))
        else:
            out_specs.append(pl.BlockSpec((1, d, tm // d, P_G), lambda i: (i // tps, 0, i % tps, 0)))
            out_shape.append(jax.ShapeDtypeStruct((bsz, d, seq // d, P_G), out_dtype))
    perms = [d for d in dils if d > 1]
    scratch = [pltpu.VMEM((tm, dm), BF16), pltpu.VMEM((dm // LANES, tm, LANES) if perms else (1, SUBLANES, LANES), F32)]
    scratch += [pltpu.VMEM((tm, dm), BF16) for _ in perms]
    w_cols = 3 * A_WIDTH + P_FB_OFF
    in_specs = [pl.BlockSpec((tm, dm), row), _resident(g.shape), _resident((dm, w_cols)), _resident(wfb.shape),
                _resident(wgt.shape)]
    args = [x, g, w, wfb, wgt]
    if gla is None:
        widths, cfg, sem = (P_B, pg_w), None, ("parallel",)
    else:
        wf, bf, gn, s0, gla_chunk = gla
        state_spec = pl.BlockSpec((1,) + s0.shape[1:], lambda i: (i // tps, 0, 0, 0))
        in_specs += [_resident(wf.shape), _resident(bf.shape), _resident(gn.shape), state_spec]
        args += [wf, bf, gn, s0]
        scratch += [pltpu.VMEM((tm, P_B), out_dtype), pltpu.VMEM(s0.shape[1:], F32)]
        widths, cfg, sem = (pg_w, B_V), (gla_chunk, tps), ("arbitrary",)
    for n in widths:
        out_specs.append(pl.BlockSpec((tm, n), row))
        out_shape.append(jax.ShapeDtypeStruct((m, n), out_dtype))
    if gla is not None:
        out_specs.append(state_spec)
        out_shape.append(jax.ShapeDtypeStruct(s0.shape, F32))
    return pl.pallas_call(
        functools.partial(_in_proj_kernel, dils=tuple(dils), chunk=512, gla=cfg),
        grid=(m // tm,),
        in_specs=in_specs,
        out_specs=out_specs,
        out_shape=out_shape,
        scratch_shapes=scratch,
        compiler_params=_cparams(sem),
        name="in_proj" if gla is None else "in_proj_gla",
    )(*args)


def _norm_matmul_kernel(x_ref, g_ref, w_ref, o_ref):
    o_ref[...] = _dot(_rms(x_ref[...], g_ref[...]).astype(BF16), w_ref[...]).astype(o_ref.dtype)


def _norm_matmul(x, g, w, tm):
    m, d = x.shape
    n = w.shape[1]
    return pl.pallas_call(
        _norm_matmul_kernel,
        grid=(m // tm,),
        in_specs=[pl.BlockSpec((tm, d), lambda i: (i, 0)), _resident(g.shape), _resident(w.shape)],
        out_specs=pl.BlockSpec((tm, n), lambda i: (i, 0)),
        out_shape=jax.ShapeDtypeStruct((m, n), F32),
        compiler_params=_cparams(("parallel",)),
        name="norm_matmul",
    )(x, g, w)


def _t5_bucket(n):
    max_exact = REL_BUCKETS // 2
    nf = jnp.maximum(n, 1).astype(F32)
    large = max_exact + (jnp.log(nf / max_exact) / math.log(REL_MAX_DIST / max_exact)
                         * (REL_BUCKETS - max_exact)).astype(jnp.int32)
    return jnp.where(n < max_exact, n, jnp.minimum(large, REL_BUCKETS - 1))


def _bias_by_steps(rel_bias, gi, dil):
    steps = jnp.arange(A_NK + 1, dtype=jnp.int32)
    return rel_bias[:, gi * A_HPG:(gi + 1) * A_HPG][_t5_bucket(steps * dil)].T.astype(F32)


def _dil_prompt_kernel(q_ref, kc_ref, kp_ref, vc_ref, vp_ref, u_ref, o_ref, l_ref, bias_scr, of_scr, lf_scr,
                       *, gi, dil, nsub):
    first = pl.program_id(1) == 0
    for h in range(A_HPG):
        bias_scr[h] = pltpu.roll(jnp.broadcast_to(u_ref[h:h + 1, :], (A_NK, 2 * A_NK)), 0, 1, stride=1, stride_axis=0)
    scale = A_DH ** -0.5
    col = lax.broadcasted_iota(jnp.int32, (A_NK, 2 * A_NK), 1)
    lane = lax.broadcasted_iota(jnp.int32, (A_NK, LANES), 1)
    for r in range(dil):
        for s in range(nsub):
            rows = slice(s * A_NK, (s + 1) * A_NK)
            dst = _strided(s * A_NK * dil + r, A_NK, dil)
            lt = jnp.zeros((A_NK, LANES), F32)
            for h in range(A_HPG):
                cols = slice(h * A_DH, (h + 1) * A_DH)
                if s == 0:
                    kw = jnp.concatenate([kp_ref[0, r, :, cols], kc_ref[0, r, rows, cols]], axis=0)
                    vw = jnp.concatenate([vp_ref[0, r, :, cols], vc_ref[0, r, rows, cols]], axis=0)
                else:
                    kw = kc_ref[0, r, (s - 1) * A_NK:(s + 1) * A_NK, cols]
                    vw = vc_ref[0, r, (s - 1) * A_NK:(s + 1) * A_NK, cols]
                sc = _dot_nt(q_ref[0, r, rows, cols], kw) * scale + bias_scr[h]
                if s == 0:
                    sc = jnp.where(jnp.logical_and(first, col < A_NK), NEG_INF, sc)
                mx = jnp.max(sc, axis=-1, keepdims=True)
                p = jnp.exp(sc - mx)
                den = jnp.sum(p, axis=-1, keepdims=True)
                o = _dot(p.astype(BF16), vw) / den
                if dil == 1:
                    o_ref[0, rows, cols] = o.astype(o_ref.dtype)
                else:
                    of_scr[h, dst, :] = o
                lt = jnp.where(lane == gi * A_HPG + h, mx + jnp.log(den), lt)
            if dil == 1:
                l_ref[0, rows, :] = lt
            else:
                lf_scr[dst, :] = lt
    if dil > 1:
        for h in range(A_HPG):
            o_ref[0, :, h * A_DH:(h + 1) * A_DH] = of_scr[h].astype(o_ref.dtype)
        l_ref[0] = lf_scr[...]


def _dil_prompt(a, u, gi, dil, bsz, seq, tt):
    m = seq // dil
    tt = min(tt, seq)
    mt = tt // dil
    nsub = mt // A_NK
    assert mt % A_NK == 0 and seq % tt == 0

    def cur(cb):
        return pl.BlockSpec((1, dil, mt, A_GW), lambda b, i: (b, 0, i, cb))

    def prev(cb):
        return pl.BlockSpec((1, dil, A_NK, A_GW), lambda b, i: (b, 0, jnp.maximum(i * nsub - 1, 0), cb))

    st = tt if dil > 1 else SUBLANES
    o, lse = pl.pallas_call(
        functools.partial(_dil_prompt_kernel, gi=gi, dil=dil, nsub=nsub),
        grid=(bsz, seq // tt),
        in_specs=[cur(0), cur(1), prev(1), cur(2), prev(2), _resident(u.shape)],
        out_specs=[pl.BlockSpec((1, tt, A_GW), lambda b, i: (b, i, 0)),
                   pl.BlockSpec((1, tt, LANES), lambda b, i: (b, i, 0))],
        out_shape=[jax.ShapeDtypeStruct((bsz, seq, A_GW), a.dtype), jax.ShapeDtypeStruct((bsz, seq, LANES), F32)],
        scratch_shapes=[pltpu.VMEM((A_HPG, A_NK, 2 * A_NK), F32), pltpu.VMEM((A_HPG, st, A_DH), F32),
                        pltpu.VMEM((st, LANES), F32)],
        compiler_params=_cparams(("parallel", "arbitrary")),
        name=f"dil_prompt_g{gi}",
    )(a, a, a, a, a, u)
    return o.reshape(bsz * seq, A_GW), lse.reshape(bsz * seq, LANES)


def _prompt_bias_row(rel_bias, gi, dil):
    tab = _bias_by_steps(rel_bias, gi, dil)
    return jnp.concatenate([tab[:, ::-1], jnp.full((A_HPG, A_NK - 1), NEG_INF, F32)], axis=1)


def _dil_sample_kernel(a1_ref, a2_ref, a3_ref, c1_ref, c2_ref, c3_ref, bc1_ref, bc2_ref, bc3_ref, bn_ref,
                       o1_ref, o2_ref, o3_ref, l1_ref, l2_ref, l3_ref):
    groups = ((a1_ref, c1_ref, bc1_ref, o1_ref, l1_ref), (a2_ref, c2_ref, bc2_ref, o2_ref, l2_ref),
              (a3_ref, c3_ref, bc3_ref, o3_ref, l3_ref))
    for j in range(a1_ref.shape[0]):
        for gi, (a_ref, c_ref, bc_ref, o_ref, l_ref) in enumerate(groups):
            _dil_sample_group(gi, a_ref.at[j], c_ref.at[j], bc_ref, bn_ref, o_ref.at[j], l_ref.at[j])


def _dil_sample_group(gi, a_ref, c_ref, bc_ref, bn_ref, o_ref, l_ref):
    scale = A_DH ** -0.5
    nq = a_ref.shape[0]
    rows = c_ref[...]
    if rows.ndim == 3:
        rows = rows.reshape(rows.shape[0] * rows.shape[1], rows.shape[2])
    rows = rows.astype(BF16)
    qs = [a_ref[:, h * A_DH:(h + 1) * A_DH] for h in range(A_HPG)]
    q_all = jnp.concatenate(qs, axis=0).astype(BF16)
    sc = _dot_nt(q_all, rows) * scale + bc_ref[...]
    scn = [_dot_nt(qs[h].astype(BF16), a_ref[:, A_GW + h * A_DH:A_GW + (h + 1) * A_DH].astype(BF16)) * scale
           + bn_ref[gi, h] for h in range(A_HPG)]
    scn = jnp.concatenate(scn, axis=0)
    mx = jnp.maximum(jnp.max(sc, axis=-1, keepdims=True), jnp.max(scn, axis=-1, keepdims=True))
    p = jnp.exp(sc - mx)
    pn = jnp.exp(scn - mx)
    den = jnp.sum(p, axis=-1, keepdims=True) + jnp.sum(pn, axis=-1, keepdims=True)
    acc = _dot(pltpu.roll(p, A_HPG, 1).astype(BF16), rows)
    lse = mx + jnp.log(den)
    lane = lax.broadcasted_iota(jnp.int32, (nq, LANES), 1)
    lt = jnp.zeros((nq, LANES), F32)
    for h in range(A_HPG):
        hr = slice(h * nq, (h + 1) * nq)
        vnew = a_ref[:, 2 * A_GW + h * A_DH:2 * A_GW + (h + 1) * A_DH].astype(BF16)
        o = (acc[hr] + _dot(pn[hr].astype(BF16), vnew)) / den[hr]
        o_ref[:, h * A_DH:(h + 1) * A_DH] = o.astype(o_ref.dtype)
        lt = jnp.where(lane == gi * A_HPG + h, lse[hr], lt)
    l_ref[...] = lt


def _toeplitz(v, n, width):
    lead, length = v.shape[:-1], v.shape[-1]
    t = jnp.tile(v, (1,) * len(lead) + (n,))[..., :n * (length - 1)]
    return t.reshape(lead + (n, length - 1))[..., :width]


def _sample_biases(rel_bias, nq, nres):
    cache, new = [], []
    for gi, (window, dil) in enumerate(A_GROUPS):
        tab = _bias_by_steps(rel_bias, gi, dil)
        length = window + nq
        rep = jnp.repeat(tab, dil, axis=1)
        rep = jnp.where((jnp.arange(rep.shape[1]) % dil == 0)[None], rep, NEG_INF)
        by = jnp.concatenate([rep, jnp.full((A_HPG, max(length - rep.shape[1], 0)), NEG_INF, F32)], axis=1)[:, :length]
        bc = _toeplitz(jnp.roll(by[:, ::-1], -(nq - 1), axis=1), nq, window)
        bn = _toeplitz(jnp.concatenate([by[:, :1], jnp.full((A_HPG, nq), NEG_INF, F32), by[:, nq - 1:0:-1]], axis=1),
                       nq, nq)
        own = (jnp.arange(2 * A_HPG)[None, :] == jnp.arange(A_HPG)[:, None])
        full = jnp.where(own[:, None, None, :], bc[:, :, :, None], NEG_INF)
        if gi == len(A_GROUPS) - 1:
            full = full.reshape(A_HPG, nq, window // dil, dil, 2 * A_HPG)[:, :, :, :nres]
        cache.append(full.reshape(A_HPG * nq, -1))
        new.append(bn)
    return cache, jnp.stack(new)


def _dil_sample(a_s, caches, rel_bias):
    bsz, nq, _ = a_s[0].shape
    w3, dil3 = caches[2].shape[1], A_GROUPS[2][1]
    nres = min(dil3, nq)
    assert w3 % dil3 == 0 and nq <= dil3, "new tokens must fall in distinct residue classes of the widest dilation"
    bcs, bn = _sample_biases(rel_bias, nq, nres)
    rpp = 2 * A_HPG
    c1 = caches[0].reshape(bsz, caches[0].shape[1] * rpp, A_DH)
    c2 = caches[1].reshape(bsz, caches[1].shape[1] * rpp, A_DH)
    c3 = caches[2].reshape(bsz, w3 // dil3, dil3 * rpp, A_DH)
    row = lambda b: (b, 0, 0)
    nb = 2 if bsz % 2 == 0 else 1
    outs = pl.pallas_call(
        _dil_sample_kernel,
        grid=(bsz // nb,),
        in_specs=[pl.BlockSpec((nb, nq, P_G), row)] * 3
        + [pl.BlockSpec((nb,) + c1.shape[1:], row), pl.BlockSpec((nb,) + c2.shape[1:], row),
           pl.BlockSpec((nb, w3 // dil3, nres * rpp, A_DH), lambda b: (b, 0, 0, 0)),
           _resident(bcs[0].shape), _resident(bcs[1].shape), _resident(bcs[2].shape), _resident(bn.shape)],
        out_specs=[pl.BlockSpec((nb, nq, A_GW), row)] * 3 + [pl.BlockSpec((nb, nq, LANES), row)] * 3,
        out_shape=[jax.ShapeDtypeStruct((bsz, nq, A_GW), F32)] * 3 + [jax.ShapeDtypeStruct((bsz, nq, LANES), F32)] * 3,
        compiler_params=_cparams(("parallel",)),
        name="dil_sample",
    )(*a_s, c1, c2, c3, bcs[0], bcs[1], bcs[2], bn)
    return [o.reshape(bsz * nq, o.shape[-1]) for o in outs]


def _gla_kernel(q_ref, k_ref, v_ref, gb_ref, fb_ref, wf_ref, bf_ref, gn_ref, s0_ref, o_ref, sf_ref, st_scr,
                *, chunk, nchunks):
    t = pl.program_id(1)

    @pl.when(t == 0)
    def _():
        st_scr[...] = s0_ref[...]

    for j in range(q_ref.shape[0]):
        _gla_tile(q_ref.at[j], k_ref.at[j], v_ref.at[j], gb_ref.at[j], fb_ref.at[j], wf_ref, bf_ref, gn_ref,
                  o_ref.at[j], st_scr.at[j], chunk, nchunks)

    @pl.when(t == pl.num_programs(1) - 1)
    def _():
        sf_ref[...] = st_scr[...]


def _gla_tile(q_ref, k_ref, v_ref, gb_ref, fb_ref, wf_ref, bf_ref, gn_ref, o_ref, st_scr, chunk, nchunks,
              before_head=None):
    tt = chunk * nchunks
    mid = chunk // 2
    z = _dot(fb_ref[...].astype(BF16), wf_ref[...]) + bf_ref[...]
    lf_all = (jnp.minimum(z, 0.0) - jnp.log(1.0 + jnp.exp(-jnp.abs(z)))) * (1.0 / B_GATE_NORM)
    blk = tt
    ri = lax.broadcasted_iota(jnp.int32, (blk, blk), 0)
    ci = lax.broadcasted_iota(jnp.int32, (blk, blk), 1)
    causal = jnp.logical_and(ri >= ci, ri // chunk == ci // chunk)
    pos = lax.broadcasted_iota(jnp.int32, (tt, B_DK), 0) % chunk
    ones_col = jnp.ones((chunk, 1), F32)
    hs = slice(0, B_DK)
    for h in range(B_HEADS):
        if before_head is not None:
            before_head(h)
        hcols = slice(h * B_DK, (h + 1) * B_DK)
        vs = slice(h * B_DV, (h + 1) * B_DV)
        lf = lf_all[:, hcols]
        b = lf
        sh = 1
        while sh < chunk:
            b = b + jnp.where(pos >= sh, pltpu.roll(b, sh, 0), 0.0)
            sh *= 2
        b3 =b.reshape(nchunks, chunk, B_DK)
        bm = jnp.broadcast_to(b3[:, mid:mid + 1, :], b3.shape).reshape(tt, B_DK)
        blr = b3[:, chunk - 1, :]
        bl = jnp.broadcast_to(b3[:, chunk - 1:chunk, :], b3.shape).reshape(tt, B_DK)
        q = q_ref[:, hcols].astype(F32) * (B_DK ** -0.5)
        k = k_ref[:, hcols].astype(F32)
        qe = (q * jnp.exp(b)).astype(BF16)
        qm = (q * jnp.exp(b - bm)).astype(BF16)
        km = (k * jnp.exp(bm - b)).astype(BF16)
        kl = (k * jnp.exp(bl - b)).astype(BF16)
        dec_t = jnp.exp(blr).T if nchunks >= SUBLANES else None
        v = v_ref[:, vs].astype(BF16)
        chunks = [slice(c * chunk, (c + 1) * chunk) for c in range(nchunks)]
        us = [_dot_tn(kl[rows, hs], v[rows]) for rows in chunks]
        o_parts = []
        for r0 in range(0, tt, blk):
            br = slice(r0, r0 + blk)
            a = jnp.where(causal, _dot_nt(qm[br, hs], km[br, hs]), 0.0)
            o_parts.append(_dot(a.astype(BF16), v[br]))
        o_intra = o_parts[0] if len(o_parts) == 1 else jnp.concatenate(o_parts, axis=0)
        gbv = gb_ref[:, vs].astype(F32)
        gate = gbv * jax.nn.sigmoid(gbv)
        st = st_scr[h]
        sts = []
        for c, rows in enumerate(chunks):
            sts.append(st.astype(BF16))
            if dec_t is not None:
                dcol = dec_t[hs, c:c + 1]
            else:
                dcol = jnp.exp(_dot_tn(lf[rows, hs], ones_col))
            st = st * dcol + us[c]
        for c, rows in enumerate(chunks):
            o = o_intra[rows] + _dot(qe[rows, hs], sts[c])
            o_ref[rows, vs] = (_rms(o, gn_ref[...]) * gate[rows]).astype(o_ref.dtype)
        st_scr[h] = st


def _gla(pb, wf, bf, gn, s0, bsz, seq, chunk, tt):
    pb3 = pb.reshape(bsz, seq, P_B)
    tt = min(tt, seq)
    nb = 4 if bsz % 4 == 0 and tt <= LANES else 1
    o, sf = pl.pallas_call(
        functools.partial(_gla_kernel, chunk=chunk, nchunks=tt // chunk),
        grid=(bsz // nb, seq // tt),
        in_specs=[pl.BlockSpec((nb, tt, B_QK), lambda b, t: (b, t, 0)),
                  pl.BlockSpec((nb, tt, B_QK), lambda b, t: (b, t, 1)),
                  pl.BlockSpec((nb, tt, B_V), lambda b, t: (b, t, 2 * B_QK // B_V)),
                  pl.BlockSpec((nb, tt, B_V), lambda b, t: (b, t, 2 * B_QK // B_V + 1)),
                  pl.BlockSpec((nb, tt, LANES), lambda b, t: (b, t, P_FB_OFF // LANES)),
                  _resident(wf.shape), _resident(bf.shape), _resident(gn.shape),
                  pl.BlockSpec((nb, B_HEADS, B_DK, B_DV), lambda b, t: (b, 0, 0, 0))],
        out_specs=[pl.BlockSpec((nb, tt, B_V), lambda b, t: (b, t, 0)),
                   pl.BlockSpec((nb, B_HEADS, B_DK, B_DV), lambda b, t: (b, 0, 0, 0))],
        out_shape=[jax.ShapeDtypeStruct((bsz, seq, B_V), pb.dtype),
                   jax.ShapeDtypeStruct((bsz, B_HEADS, B_DK, B_DV), F32)],
        scratch_shapes=[pltpu.VMEM((nb, B_HEADS, B_DK, B_DV), F32)],
        compiler_params=_cparams(("parallel", "arbitrary")),
        name="gla",
    )(pb3, pb3, pb3, pb3, pb3, wf, bf, gn, s0)
    return o.reshape(bsz * seq, B_V), sf


def _mix_out_kernel(o1_ref, o2_ref, o3_ref, l1_ref, l2_ref, l3_ref, ob_ref, ga_ref, gbt_ref, x_ref,
                    ex_ref, wa_ref, wb_ref, wo_ref, gp_ref, y_ref, *, nsplit):
    n = x_ref.shape[0] // nsplit
    subs = []
    for i in range(nsplit):
        rows = slice(i * n, (i + 1) * n)
        subs.append(_mix_out_rows(*(r.at[rows] for r in (o1_ref, o2_ref, o3_ref, l1_ref, l2_ref, l3_ref, ob_ref, ga_ref,
                                                         gbt_ref, x_ref)),
                                  ex_ref, wa_ref, wb_ref, wo_ref, gp_ref, y_ref.at[rows]))
    live = []
    while subs or live:
        if subs:
            live.append(subs.pop(0))
        live = [g for g in live if next(g, "done") != "done"]


def _mix_out_rows(o1_ref, o2_ref, o3_ref, l1_ref, l2_ref, l3_ref, ob_ref, ga_ref, gbt_ref, x_ref,
                  ex_ref, wa_ref, wb_ref, wo_ref, gp_ref, y_ref):
    yb = _dot(ob_ref[...].astype(BF16), wb_ref[...])
    yield
    ng = len(A_GROUPS)
    ls = l1_ref[...] + l2_ref[...] + l3_ref[...]
    lane = lax.broadcasted_iota(jnp.int32, ls.shape, 1)

    def over_groups(v, op):
        red = v
        for g in range(1, ng):
            red = op(red, pltpu.roll(v, LANES - g * A_HPG, 1))
        out = red
        for g in range(1, ng):
            out = jnp.where(lane >= g * A_HPG, pltpu.roll(red, g * A_HPG, 1), out)
        return out

    e = jnp.exp(ls - over_groups(ls, jnp.maximum))
    w = jnp.where(lane < ng * A_HPG, e / over_groups(e, jnp.add), 0.0)
    yield
    wf = _dot(w.astype(BF16), ex_ref[...])
    yield
    comb = (wf[:, :A_GW] * o1_ref[...].astype(F32) + wf[:, A_GW:2 * A_GW] * o2_ref[...].astype(F32)
            + wf[:, 2 * A_GW:] * o3_ref[...].astype(F32)).astype(BF16)
    yield
    ya = _dot(comb, wa_ref[...])
    yield
    mix = (jax.nn.sigmoid(ga_ref[...].astype(F32)) * ya + jax.nn.sigmoid(gbt_ref[...].astype(F32)) * yb).astype(BF16)
    yield
    out = _dot(mix, wo_ref[...])
    yield
    y_ref[...] = x_ref[...] + _rms(out, gp_ref[...])


def _mix_out(oas, lses, ob, gates, x, wa, wb, wo, gpost, tm):
    m, d = x.shape
    row = lambda i: (i, 0)
    ex = (jnp.arange(LANES)[:, None] == jnp.arange(A_WIDTH)[None, :] // A_DH).astype(BF16)
    return pl.pallas_call(
        functools.partial(_mix_out_kernel, nsplit=2 if tm % (2 * 2 * SUBLANES) == 0 else 1),
        grid=(m // tm,),
        in_specs=[pl.BlockSpec((tm, A_GW), row)] * 3 + [pl.BlockSpec((tm, LANES), row)] * 3
        + [pl.BlockSpec((tm, B_V), row), pl.BlockSpec((tm, d), row), pl.BlockSpec((tm, d), lambda i: (i, 1)),
           pl.BlockSpec((tm, d), row), _resident(ex.shape), _resident(wa.shape), _resident(wb.shape),
           _resident(wo.shape), _resident(gpost.shape)],
        out_specs=pl.BlockSpec((tm, d), row),
        out_shape=jax.ShapeDtypeStruct((m, d), F32),
        compiler_params=_cparams(("parallel",)),
        name="mix_out",
    )(*oas, *lses, ob, gates, gates, x, ex, wa, wb, wo, gpost)


def _cross_kernel(x_ref, kv_ref, wq_ref, wo_ref, gpre_ref, gpost_ref, y_ref, q_scr, c_scr, *, nb, nsplit):
    tm = x_ref.shape[0]
    if nb == 1:
        n = tm // nsplit
        for i in range(nsplit):
            _cross_rows(x_ref, kv_ref, wq_ref, wo_ref, gpre_ref, gpost_ref, y_ref, q_scr, c_scr,
                        slice(i * n, (i + 1) * n), [(0, slice(i * n, (i + 1) * n))])
    else:
        rpb = tm // nb
        _cross_rows(x_ref, kv_ref, wq_ref, wo_ref, gpre_ref, gpost_ref, y_ref, q_scr, c_scr,
                    slice(0, tm), [(j, slice(j * rpb, (j + 1) * rpb)) for j in range(nb)])


def _cross_rows(x_ref, kv_ref, wq_ref, wo_ref, gpre_ref, gpost_ref, y_ref, q_scr, c_scr, tile, batches):
    x = x_ref[tile]
    d = x.shape[1]
    dh = d // M_HEADS
    q_scr[tile] = _dot(_rms(x, gpre_ref[...]).astype(BF16), wq_ref[...])
    scale = dh ** -0.5
    for j, rows in batches:
        rpb = rows.stop - rows.start
        if len(kv_ref.shape) == 5:
            mem = kv_ref.shape[1]
            kall = kv_ref[j, :, 0].reshape(mem * M_HEADS, dh).astype(BF16)
            vall = kv_ref[j, :, 1].reshape(mem * M_HEADS, dh).astype(BF16)
            qall = jnp.concatenate([q_scr[rows, h * dh:(h + 1) * dh] for h in range(M_HEADS)], axis=0).astype(BF16)
            rh = lax.broadcasted_iota(jnp.int32, (M_HEADS * rpb, mem * M_HEADS), 0) // rpb
            ch = lax.broadcasted_iota(jnp.int32, (M_HEADS * rpb, mem * M_HEADS), 1) % M_HEADS
            sc = jnp.where(rh == ch, _dot_nt(qall, kall) * scale, NEG_INF)
            p = jnp.exp(sc - jnp.max(sc, axis=-1, keepdims=True))
            ctx = _dot(p.astype(BF16), vall) / jnp.sum(p, axis=-1, keepdims=True)
            for h in range(M_HEADS):
                c_scr[rows, h * dh:(h + 1) * dh] = ctx[h * rpb:(h + 1) * rpb]
            continue
        for h in range(M_HEADS):
            cols = slice(h * dh, (h + 1) * dh)
            k = kv_ref[j, :, cols].astype(BF16)
            v = kv_ref[j, :, d + cols.start:d + cols.stop].astype(BF16)
            sc = _dot_nt(q_scr[rows, cols].astype(BF16), k) * scale
            p = jnp.exp(sc - jnp.max(sc, axis=-1, keepdims=True))
            den = jnp.sum(p, axis=-1, keepdims=True)
            c_scr[rows, cols] = _dot(p.astype(BF16), v) / den
    out = _dot(c_scr[tile].astype(BF16), wo_ref[...])
    y_ref[tile] = x + _rms(out, gpost_ref[...])


def _cross(x, kv, wq, wo, gpre, gpost, tm, nb, rows_per_batch):
    m, d = x.shape
    tiles_per_kv = max(rows_per_batch // tm, 1)
    kv_tail = (0,) * (kv.ndim - 1)
    return pl.pallas_call(
        functools.partial(_cross_kernel, nb=nb, nsplit=2 if tm >= 1024 else 1),
        grid=(m // tm,),
        in_specs=[pl.BlockSpec((tm, d), lambda i: (i, 0)),
                  pl.BlockSpec((nb,) + kv.shape[1:], lambda i: (i // tiles_per_kv,) + kv_tail),
                  _resident(wq.shape), _resident(wo.shape), _resident(gpre.shape), _resident(gpost.shape)],
        out_specs=pl.BlockSpec((tm, d), lambda i: (i, 0)),
        out_shape=jax.ShapeDtypeStruct((m, d), F32),
        scratch_shapes=[pltpu.VMEM((tm, d), F32), pltpu.VMEM((tm, d), F32)],
        compiler_params=_cparams(("parallel",)),
        name="cross_attn",
    )(x, kv, wq, wo, gpre, gpost)


def _ffn_kernel(x_ref, wg_ref, wu_ref, wd_ref, gpre_ref, gpost_ref, y_ref, h_scr, *, chunk, nsplit):
    n = x_ref.shape[0] // nsplit
    for i in range(nsplit):
        rows = slice(i * n, (i + 1) * n)
        x = x_ref[rows]
        h_scr[rows] = _rms(x, gpre_ref[...]).astype(BF16)
        acc = jnp.zeros(x.shape, F32)
        for c0 in range(0, wg_ref.shape[1], chunk):
            g = _dot(h_scr[rows], wg_ref[:, c0:c0 + chunk])
            u = _dot(h_scr[rows], wu_ref[:, c0:c0 + chunk])
            acc = acc + _dot((g * jax.nn.sigmoid(g) * u).astype(BF16), wd_ref[c0:c0 + chunk, :])
        y_ref[rows] = x + _rms(acc, gpost_ref[...])


def _ffn(x, wg, wu, wd, gpre, gpost, tm):
    m, d = x.shape
    return pl.pallas_call(
        functools.partial(_ffn_kernel, chunk=256, nsplit=1),
        grid=(m // tm,),
        in_specs=[pl.BlockSpec((tm, d), lambda i: (i, 0)), _resident(wg.shape), _resident(wu.shape),
                  _resident(wd.shape), _resident(gpre.shape), _resident(gpost.shape)],
        out_specs=pl.BlockSpec((tm, d), lambda i: (i, 0)),
        out_shape=jax.ShapeDtypeStruct((m, d), F32),
        scratch_shapes=[pltpu.VMEM((tm, d), BF16)],
        compiler_params=_cparams(("parallel",)),
        name="ffn",
    )(x, wg, wu, wd, gpre, gpost)


def _row(v):
    return v.reshape(1, -1).astype(F32)


def _kv_tail_kernel(k_ref, v_ref, o_ref, *, dil):
    n = k_ref.shape[2]
    rpp = 2 * A_HPG
    for part, ref in enumerate((k_ref, v_ref)):
        for h in range(A_HPG):
            for r in range(dil):
                piece = ref[0, r, :, h * A_DH:(h + 1) * A_DH].astype(F32)
                o_ref[0, _strided(r * rpp + part * A_HPG + h, n, rpp * dil), :] = piece


def _kv_tail(a4, keep, tt):
    bsz, dil, m, _ = a4.shape
    tt = min(tt, keep)
    n = tt // dil
    first = (m - keep // dil) // n
    rpp = 2 * A_HPG

    def src(cb):
        return pl.BlockSpec((1, dil, n, A_GW), lambda b, i: (b, 0, first + i, cb))

    out = pl.pallas_call(
        functools.partial(_kv_tail_kernel, dil=dil),
        grid=(bsz, keep // tt),
        in_specs=[src(1), src(2)],
        out_specs=pl.BlockSpec((1, tt * rpp, A_DH), lambda b, i: (b, i, 0)),
        out_shape=jax.ShapeDtypeStruct((bsz, keep * rpp, A_DH), F32),
        compiler_params=_cparams(("parallel", "parallel")),
        name=f"kv_tail_d{dil}",
    )(a4, a4)
    return out.reshape(bsz, keep, 2, A_HPG, A_DH)


def _kv_rows(a, keep):
    bsz, seq, _ = a.shape
    return a[:, seq - keep:, A_GW:].astype(F32).reshape(bsz, keep, 2, A_HPG, A_DH)


def _trunk(x, mem_kv, s0, chunk, caches, rel_bias, lw, *, tm, act_dtype, cross_tm, cross_nb):
    (g_mix_pre, g_mix_post, w_in, w_f2, b_f2, gla_norm, w_pa, w_pb, w_out,
     g_mem_pre, g_mem_post, w_mq, w_mo, g_ffn_pre, g_ffn_post, w_gate, w_up, w_down) = lw
    bsz, seq, d = x.shape
    x2 = x.reshape(bsz * seq, d)
    if caches is None:
        dils = tuple(dil for _, dil in A_GROUPS)
        *a_g, pg, ob, s_new = _in_proj(x2, g_mix_pre, w_in, bsz, seq, tm, dils, act_dtype,
                                       gla=(w_f2, b_f2, gla_norm, s0, chunk))
        oas, lses, wins = [], [], []
        for gi, (window, dil) in enumerate(A_GROUPS):
            a4 = a_g[gi].reshape(bsz, dil, seq // dil, P_G)
            o, l = _dil_prompt(a4, _prompt_bias_row(rel_bias, gi, dil), gi, dil, bsz, seq, A_NK * A_GROUPS[-1][1])
            oas.append(o)
            lses.append(l)
            keep = min(window, seq)
            wins.append(_kv_tail(a4, keep, 512))
    else:
        *a_g, pb, pg = _in_proj(x2, g_mix_pre, w_in, bsz, seq, tm, (1,) * len(A_GROUPS), act_dtype)
        a_s = [a.reshape(bsz, seq, P_G) for a in a_g]
        res = _dil_sample(a_s, caches, rel_bias)
        oas, lses = res[:3], res[3:]
        wins = [_kv_rows(a, seq) for a in a_s]
        ob, s_new = _gla(pb, w_f2, b_f2, gla_norm, s0, bsz, seq, chunk, seq)
    x2 = _mix_out(oas, lses, ob, pg, x2, w_pa, w_pb, w_out, g_mix_post, min(2 * tm, bsz * seq))
    x2 = _cross(x2, mem_kv, w_mq, w_mo, g_mem_pre, g_mem_post, cross_tm, cross_nb, seq)
    x2 = _ffn(x2, w_gate, w_up, w_down, g_ffn_pre, g_ffn_post, min(2 * tm, bsz * seq))
    return x2.reshape(bsz, seq, d), wins, s_new


def kernel(x_prompt, x_sample, cache_win1_kv, cache_win2_kv, cache_win3_kv, state_gla, cache_mem_kv, mem_prompt, rel_bias, norm_mix_pre, norm_mix_post, w_in, w_f2, b_f2, gla_norm, w_proj_a, w_proj_b, w_out, norm_memtok, w_mk, w_mv, norm_mem_pre, norm_mem_post, w_mq, w_mo, norm_ffn_pre, norm_ffn_post, w_ffn_gate, w_ffn_up, w_ffn_down):
    depth = w_in.shape[0]
    bsz, seq, d = x_prompt.shape
    dbs, dseq, _ = x_sample.shape
    mem_len = mem_prompt.shape[1]
    yp, ys = x_prompt, x_sample
    outs = [[] for _ in range(9)]
    for l in range(depth):
        wl = w_in[l]
        f0 = 3 * A_WIDTH + P_FB_OFF
        wlb = wl.astype(BF16)
        w_in_l = (wlb, jnp.pad(wlb[:, f0:f0 + B_RANK], ((0, 0), (0, LANES - B_RANK))), wlb[:, f0 + B_RANK:])
        w_f2_l = jnp.pad(w_f2[l], ((0, LANES - B_RANK), (0, 0))).astype(BF16)
        lw = (_row(norm_mix_pre[l]), _row(norm_mix_post[l]), w_in_l, w_f2_l, _row(b_f2[l]), _row(gla_norm[l]),
              w_proj_a[l].astype(BF16), w_proj_b[l].astype(BF16), w_out[l].astype(BF16),
              _row(norm_mem_pre[l]), _row(norm_mem_post[l]), w_mq[l].astype(BF16), w_mo[l].astype(BF16),
              _row(norm_ffn_pre[l]), _row(norm_ffn_post[l]),
              w_ffn_gate[l].astype(BF16), w_ffn_up[l].astype(BF16), w_ffn_down[l].astype(BF16))
        w_mkv = jnp.concatenate([w_mk[l], w_mv[l]], axis=1).astype(BF16)
        mem_kv_p = _norm_matmul(mem_prompt.reshape(bsz * mem_len, d), _row(norm_memtok[l]), w_mkv,
                                min(256, bsz * mem_len)).reshape(bsz, mem_len, 2 * d)
        s0 = jnp.zeros((bsz, B_HEADS, B_DK, B_DV), F32)
        tm_p = min(512, seq)
        yp, win_p, gla_p = _trunk(yp, mem_kv_p, s0, min(B_CHUNK, seq), None, rel_bias, lw,
                                  tm=tm_p, act_dtype=BF16, cross_tm=min(2 * tm_p, seq), cross_nb=1)
        caches = (cache_win1_kv[l], cache_win2_kv[l], cache_win3_kv[l])
        cross_nb = min(4, dbs)
        ys, win_s, gla_s = _trunk(ys, cache_mem_kv[l], state_gla[l], dseq, caches,
                                  rel_bias, lw, tm=dbs * dseq, act_dtype=F32,
                                  cross_tm=cross_nb * dseq, cross_nb=cross_nb)
        for lst, v in zip(outs, (*win_p, gla_p, mem_kv_p.reshape(bsz, mem_len, 2, M_HEADS, d // M_HEADS), *win_s, gla_s)):
            lst.append(v)
    return (yp, ys, *[jnp.stack(o) for o in outs])
```
